```python
import jax, jax.numpy as jnp
from jax import lax
import numpy as np

D_MODEL = 1024
BATCH = 4
SEQ = 4096
DEPTH = 4
DEC_BATCH = 32
DEC_SEQ = 8
PAST_LEN = 8192
PAGE_SIZE = 128

N_META = 16
N_A = DEPTH // 2
N_B = DEPTH - N_A
RET_HEADS = 4
RET_DK = D_MODEL // RET_HEADS
RET_DV = 2 * D_MODEL // RET_HEADS
RET_IN_WIDTH = 2 * RET_HEADS * RET_DK + 2 * RET_HEADS * RET_DV
RET_CHUNK = 128
ROPE_BASE = 10000.0
FOX_HEADS = 16
FOX_DH = D_MODEL // FOX_HEADS
FOX_SCALE = FOX_DH ** -0.5
Q_BLOCK = 128
D_FF = 4 * D_MODEL
EPS = 1e-6
MASK_VALUE = -1e30

kernel_name = 'yoco_retention_fox_decoder'


def rms_norm(x, g=None):
    xf = x.astype(jnp.float32)
    y = xf * lax.rsqrt(jnp.mean(xf * xf, axis=-1, keepdims=True) + EPS)
    if g is not None:
        y = y * g.astype(jnp.float32)
    return y.astype(x.dtype)


def rotary(x, pos):
    half = x.shape[-1] // 2
    inv_freq = ROPE_BASE ** (-jnp.arange(half, dtype=jnp.float32) / half)
    ang = pos.astype(jnp.float32)[:, None] * inv_freq[None, :]
    cos = jnp.cos(ang)[:, None, :]
    sin = jnp.sin(ang)[:, None, :]
    xf = x.astype(jnp.float32)
    x1, x2 = xf[..., :half], xf[..., half:]
    return jnp.concatenate([x1 * cos - x2 * sin, x2 * cos + x1 * sin], axis=-1).astype(x.dtype)


def retention_log_decay():
    return jnp.log1p(-jnp.exp2(-5.0 - jnp.arange(RET_HEADS, dtype=jnp.float32)))


def retention_chunk(S, q, k, v):
    S = S.astype(jnp.float32)
    qf, kf, vf = q.astype(jnp.float32), k.astype(jnp.float32), v.astype(jnp.float32)
    C = q.shape[1]
    lg = retention_log_decay()
    idx = jnp.arange(C, dtype=jnp.float32)
    diff = idx[:, None] - idx[None, :]
    decay = jnp.where(diff >= 0, jnp.exp(lg[:, None, None] * jnp.maximum(diff, 0.0)), 0.0)
    scores = jnp.einsum('bihd,bjhd->bhij', qf, kf) * decay[None]
    intra = jnp.einsum('bhij,bjhe->bihe', scores, vf)
    q_decay = jnp.exp(lg[None, :] * (idx[:, None] + 1.0))
    cross = jnp.einsum('bihd,bhde->bihe', qf, S) * q_decay[None, :, :, None]
    k_decay = jnp.exp(lg[None, :] * (C - 1.0 - idx[:, None]))
    S_new = jnp.exp(lg * C)[None, :, None, None] * S + jnp.einsum('bjhd,bjhe->bhde', kf * k_decay[None, :, :, None], vf)
    return intra + cross, S_new


def retention_prompt(q, k, v):
    B, T = q.shape[:2]
    nc = T // RET_CHUNK

    def to_chunks(a):
        return jnp.moveaxis(a.reshape(B, nc, RET_CHUNK, *a.shape[2:]), 1, 0)

    def step(S, qkv):
        o, S = retention_chunk(S, *qkv)
        return S, o

    S0 = jnp.zeros((B, RET_HEADS, RET_DK, RET_DV), jnp.float32)
    S, o = lax.scan(step, S0, (to_chunks(q), to_chunks(k), to_chunks(v)))
    return jnp.moveaxis(o, 0, 1).reshape(B, T, RET_HEADS, RET_DV), S


def retention_in(x, pos, g, w_in):
    B, T, _ = x.shape
    p = rms_norm(x, g) @ w_in
    q, k, v, gate = jnp.split(p, [RET_HEADS * RET_DK, 2 * RET_HEADS * RET_DK, 2 * RET_HEADS * RET_DK + RET_HEADS * RET_DV], axis=-1)
    q = rotary(q.reshape(B, T, RET_HEADS, RET_DK), pos)
    k = rotary(k.reshape(B, T, RET_HEADS, RET_DK), pos) * (RET_DK ** -0.5)
    v = v.reshape(B, T, RET_HEADS, RET_DV)
    return q, k, v, gate


def retention_out(o, gate, w_out):
    B, T = o.shape[:2]
    o = rms_norm(o).reshape(B, T, RET_HEADS * RET_DV).astype(gate.dtype)
    return (jax.nn.silu(gate) * o) @ w_out


def fox_shared_kv(h, g_kv, w_kvf, b_f, g_k):
    B, T, _ = h.shape
    p = rms_norm(h, g_kv) @ w_kvf
    k, v, f = jnp.split(p, [D_MODEL, 2 * D_MODEL], axis=-1)
    k = rms_norm(k.reshape(B, T, FOX_HEADS, FOX_DH), g_k)
    v = v.reshape(B, T, FOX_HEADS, FOX_DH)
    logf = jax.nn.log_sigmoid((f + b_f).astype(jnp.float32))
    return k, v, logf


def fox_in(x, g, w_qg, g_q):
    B, T, _ = x.shape
    p = rms_norm(x, g) @ w_qg
    q, gate = jnp.split(p, [D_MODEL], axis=-1)
    q = rms_norm(q.reshape(B, T, FOX_HEADS, FOX_DH), g_q)
    return q, gate


def fox_out(o, gate, w_o):
    B, T = o.shape[:2]
    return (o.reshape(B, T, D_MODEL) * jax.nn.sigmoid(gate)) @ w_o


def fox_prompt_attention(q, k, v, cum, valid):
    B, T = q.shape[:2]
    nb = T // Q_BLOCK
    kpos = jnp.arange(T)
    cum_k = jnp.swapaxes(cum, 1, 2)

    def blocks(a):
        return jnp.moveaxis(a.reshape(B, nb, Q_BLOCK, *a.shape[2:]), 1, 0)

    def one_block(args):
        qi, ci, pi = args
        s = jnp.einsum('bqhd,bkhd->bhqk', qi, k).astype(jnp.float32) * FOX_SCALE
        s = s + jnp.swapaxes(ci, 1, 2)[..., :, None] - cum_k[:, :, None, :]
        mask = (kpos[None, :] <= pi[:, None]) & valid[None, :]
        p = jax.nn.softmax(jnp.where(mask, s, MASK_VALUE), axis=-1)
        return jnp.einsum('bhqk,bkhd->bqhd', p.astype(v.dtype), v)

    o = lax.map(one_block, (blocks(q), blocks(cum), kpos.reshape(nb, Q_BLOCK)))
    return jnp.moveaxis(o, 0, 1).reshape(B, T, FOX_HEADS, FOX_DH)


def fox_sample_attention(q, k_past, v_past, cum_past, k_new, v_new, cum_new):
    DS = q.shape[1]
    P = k_past.shape[1]
    s = jnp.concatenate([jnp.einsum('bqhd,bkhd->bhqk', q, k_past),
                         jnp.einsum('bqhd,bkhd->bhqk', q, k_new)], axis=-1).astype(jnp.float32) * FOX_SCALE
    cum_k = jnp.swapaxes(jnp.concatenate([cum_past, cum_new], axis=1), 1, 2)
    s = s + jnp.swapaxes(cum_new, 1, 2)[..., :, None] - cum_k[:, :, None, :]
    causal = jnp.arange(DS)[None, :] <= jnp.arange(DS)[:, None]
    mask = jnp.concatenate([jnp.ones((DS, P), dtype=bool), causal], axis=1)
    p = jax.nn.softmax(jnp.where(mask, s, MASK_VALUE), axis=-1).astype(v_new.dtype)
    return (jnp.einsum('bhqk,bkhd->bqhd', p[..., :P], v_past)
            + jnp.einsum('bhqk,bkhd->bqhd', p[..., P:], v_new))


def mlp(x, g, w_up, w_down):
    return jnp.square(jax.nn.relu(rms_norm(x, g) @ w_up)) @ w_down


def prompt_forward(x_prompt, meta, g_attn, g_mlp, w_ret_in, w_ret_out, g_kv, w_kvf, b_f, g_k,
                   w_fox_qg, g_q, w_fox_out, w_mlp_up, w_mlp_down):
    B = x_prompt.shape[0]
    pad = RET_CHUNK - N_META
    x = jnp.concatenate([jnp.zeros((B, pad, D_MODEL), x_prompt.dtype),
                         jnp.broadcast_to(meta[None], (B, N_META, D_MODEL)).astype(x_prompt.dtype),
                         x_prompt], axis=1)
    T = x.shape[1]
    pos = jnp.arange(T) - pad
    valid = pos >= 0
    ret_states = []
    for l in range(DEPTH):
        if l < N_A:
            q, k, v, gate = retention_in(x, pos, g_attn[l], w_ret_in[l])
            k = k * valid[None, :, None, None].astype(k.dtype)
            o, S = retention_prompt(q, k, v)
            ret_states.append(S)
            x = x + retention_out(o, gate, w_ret_out[l])
        else:
            if l == N_A:
                k_sh, v_sh, logf = fox_shared_kv(x, g_kv, w_kvf, b_f, g_k)
                logf = jnp.where(valid[None, :, None], logf, 0.0)
                cum = jnp.cumsum(logf, axis=1)
            q, gate = fox_in(x, g_attn[l], w_fox_qg[l - N_A], g_q[l - N_A])
            o = fox_prompt_attention(q, k_sh, v_sh, cum, valid)
            x = x + fox_out(o, gate, w_fox_out[l - N_A])
        x = x + mlp(x, g_mlp[l], w_mlp_up[l], w_mlp_down[l])
    y = x[:, pad + N_META:]
    return y, jnp.stack(ret_states), k_sh[:, pad:], v_sh[:, pad:], logf[:, pad:]


def sample_forward(x, state_ret, cache_k, cache_v, cache_logf, page_table, g_attn, g_mlp, w_ret_in,
                   w_ret_out, g_kv, w_kvf, b_f, g_k, w_fox_qg, g_q, w_fox_out, w_mlp_up, w_mlp_down):
    DB, DS, _ = x.shape
    P = page_table.shape[1] * cache_k.shape[1]
    pos = P + jnp.arange(DS)
    ret_states = []
    for l in range(DEPTH):
        if l < N_A:
            q, k, v, gate = retention_in(x, pos, g_attn[l], w_ret_in[l])
            o, S = retention_chunk(state_ret[l], q, k, v)
            ret_states.append(S)
            x = x + retention_out(o, gate, w_ret_out[l])
        else:
            if l == N_A:
                k_new, v_new, logf_new = fox_shared_kv(x, g_kv, w_kvf, b_f, g_k)
                k_past = cache_k[page_table].reshape(DB, P, FOX_HEADS, FOX_DH)
                v_past = cache_v[page_table].reshape(DB, P, FOX_HEADS, FOX_DH)
                cum_past = jnp.cumsum(cache_logf[page_table].reshape(DB, P, FOX_HEADS).astype(jnp.float32), axis=1)
                cum_new = cum_past[:, -1:] + jnp.cumsum(logf_new, axis=1)
            q, gate = fox_in(x, g_attn[l], w_fox_qg[l - N_A], g_q[l - N_A])
            o = fox_sample_attention(q, k_past, v_past, cum_past, k_new, v_new, cum_new)
            x = x + fox_out(o, gate, w_fox_out[l - N_A])
        x = x + mlp(x, g_mlp[l], w_mlp_up[l], w_mlp_down[l])
    return x, jnp.stack(ret_states), k_new, v_new, logf_new


def setup_inputs(seed: int = 0) -> dict:
    key = jax.random.key(seed)
    ks = jax.random.split(key, 24)
    f32 = jnp.float32
    n_pages = PAST_LEN // PAGE_SIZE
    n_used = DEC_BATCH * n_pages
    n_pool = n_used + n_used // 4

    def nrm(k, shape, scale=1.0):
        return jax.random.normal(k, shape, f32) * scale

    return {
        'x_prompt': nrm(ks[0], (BATCH, SEQ, D_MODEL)),
        'x_sample': nrm(ks[1], (DEC_BATCH, DEC_SEQ, D_MODEL)),
        'state_ret': nrm(ks[2], (N_A, DEC_BATCH, RET_HEADS, RET_DK, RET_DV), 0.5),
        'cache_k': nrm(ks[3], (n_pool, PAGE_SIZE, FOX_HEADS, FOX_DH)),
        'cache_v': nrm(ks[4], (n_pool, PAGE_SIZE, FOX_HEADS, FOX_DH)),
        'cache_logf': jax.nn.log_sigmoid(4.0 + nrm(ks[5], (n_pool, PAGE_SIZE, FOX_HEADS))),
        'page_table': jax.random.permutation(ks[6], n_pool)[:n_used].reshape(DEC_BATCH, n_pages).astype(jnp.int32),
        'meta': nrm(ks[7], (N_META, D_MODEL)),
        'g_attn': 1.0 + 0.02 * nrm(ks[8], (DEPTH, D_MODEL)),
        'g_mlp': 1.0 + 0.02 * nrm(ks[9], (DEPTH, D_MODEL)),
        'w_ret_in': nrm(ks[10], (N_A, D_MODEL, RET_IN_WIDTH), D_MODEL ** -0.5),
        'w_ret_out': nrm(ks[11], (N_A, RET_HEADS * RET_DV, D_MODEL), (RET_HEADS * RET_DV) ** -0.5),
        'g_kv': 1.0 + 0.02 * nrm(ks[12], (D_MODEL,)),
        'w_kvf': nrm(ks[13], (D_MODEL, 2 * D_MODEL + FOX_HEADS), D_MODEL ** -0.5),
        'b_f': jax.random.uniform(ks[14], (FOX_HEADS,), f32, 2.0, 6.0),
        'g_k': 1.0 + 0.02 * nrm(ks[15], (FOX_DH,)),
        'w_fox_qg': nrm(ks[16], (N_B, D_MODEL, 2 * D_MODEL), D_MODEL ** -0.5),
        'g_q': 1.0 + 0.02 * nrm(ks[17], (N_B, FOX_DH)),
        'w_fox_out': nrm(ks[18], (N_B, D_MODEL, D_MODEL), D_MODEL ** -0.5),
        'w_mlp_up': nrm(ks[19], (DEPTH, D_MODEL, D_FF), D_MODEL ** -0.5),
        'w_mlp_down': nrm(ks[20], (DEPTH, D_FF, D_MODEL), D_FF ** -0.5),
    }


def reference(x_prompt, x_sample, state_ret, cache_k, cache_v, cache_logf, page_table, meta, g_attn, g_mlp,
              w_ret_in, w_ret_out, g_kv, w_kvf, b_f, g_k, w_fox_qg, g_q, w_fox_out, w_mlp_up, w_mlp_down):
    y_prompt, state_ret_prompt, k_prompt, v_prompt, logf_prompt = prompt_forward(
        x_prompt, meta, g_attn, g_mlp, w_ret_in, w_ret_out, g_kv, w_kvf, b_f, g_k,
        w_fox_qg, g_q, w_fox_out, w_mlp_up, w_mlp_down)
    y_sample, state_ret_sample, k_sample, v_sample, logf_sample = sample_forward(
        x_sample, state_ret, cache_k, cache_v, cache_logf, page_table, g_attn, g_mlp, w_ret_in,
        w_ret_out, g_kv, w_kvf, b_f, g_k, w_fox_qg, g_q, w_fox_out, w_mlp_up, w_mlp_down)
    return (y_prompt, y_sample, state_ret_prompt, state_ret_sample, k_prompt, v_prompt, logf_prompt,
            k_sample, v_sample, logf_sample)
```

```python
import functools

import jax
import jax.numpy as jnp
from jax import lax
from jax.experimental import pallas as pl
from jax.experimental.pallas import tpu as pltpu

F32 = jnp.float32
BF16 = jnp.bfloat16

N_META = 16
RET_HEADS = 4
RET_CHUNK = 128
ROPE_BASE = 10000.0
FOX_HEADS = 16
FOX_DH = 64
EPS = 1e-6
MASK_VALUE = -1e30

LANES = 128
MXU_DIM = 256
VMEM_LIMIT = 56 * 1024 * 1024
PAGES_PER_STEP = 8


def _cparams(sem):
    return pltpu.CompilerParams(dimension_semantics=sem, vmem_limit_bytes=VMEM_LIMIT)


def _tile(n, target, mult=8):
    best = None
    for t in range(mult, min(n, target) + 1, mult):
        if n % t == 0:
            best = t
    assert best is not None, (n, target, mult)
    return best


def _rms(x, g_row):
    ms = jnp.mean(x * x, axis=-1, keepdims=True)
    return x * lax.rsqrt(ms + EPS) * g_row


def _split2(x):
    hi = x.astype(BF16)
    lo = (x - hi.astype(F32)).astype(BF16)
    return hi, lo


def _split3(x):
    hi = x.astype(BF16)
    r = x - hi.astype(F32)
    mid = r.astype(BF16)
    lo = (r - mid.astype(F32)).astype(BF16)
    return hi, mid, lo


def _head_rms(x, bd_ref, g_row):
    xx = x * x
    hi, lo = _split2(xx)
    bd = bd_ref[...]
    parts = []
    for c in range(x.shape[1] // MXU_DIM):
        sl = slice(c * MXU_DIM, (c + 1) * MXU_DIM)
        parts.append(jnp.dot(hi[:, sl], bd, preferred_element_type=F32)
                     + jnp.dot(lo[:, sl], bd, preferred_element_type=F32))
    ss = jnp.concatenate(parts, axis=-1)
    return x * lax.rsqrt(ss * (1.0 / FOX_DH) + EPS) * g_row


def _norm_matmul_kernel(x_ref, g_ref, w_ref, o_ref, xn_ref):
    @pl.when(pl.program_id(1) == 0)
    def _():
        xn_ref[...] = _rms(x_ref[...], g_ref[...]).astype(BF16)

    o_ref[...] = jnp.dot(xn_ref[...], w_ref[...], preferred_element_type=F32)


def norm_matmul(x, g, w):
    n, d = x.shape
    nout = w.shape[1]
    tm = _tile(n, 1536)
    tn = _tile(nout, 1024, LANES)
    return pl.pallas_call(
        _norm_matmul_kernel,
        grid=(n // tm, nout // tn),
        in_specs=[
            pl.BlockSpec((tm, d), lambda i, j: (i, 0)),
            pl.BlockSpec((1, d), lambda i, j: (0, 0)),
            pl.BlockSpec((d, tn), lambda i, j: (0, j)),
        ],
        out_specs=pl.BlockSpec((tm, tn), lambda i, j: (i, j)),
        out_shape=jax.ShapeDtypeStruct((n, nout), F32),
        scratch_shapes=[pltpu.VMEM((tm, d), BF16)],
        compiler_params=_cparams(("parallel", "arbitrary")),
        name="norm_matmul",
    )(x, g.reshape(1, d), w)


def _retention_kernel(q_ref, k_ref, v_ref, gt_ref, cq_ref, sq_ref, ck_ref, sk_ref,
                      dm_ref, qd_ref, kd_ref, gc_ref, s0_ref, og_ref, s_ref):
    @pl.when(pl.program_id(2) == 0)
    def _():
        s_ref[...] = s0_ref[...]

    half = q_ref.shape[1] // 2
    q = q_ref[...]
    k = k_ref[...]
    q1, q2 = q[:, :half], q[:, half:]
    k1, k2 = k[:, :half], k[:, half:]
    cq, sq = cq_ref[...], sq_ref[...]
    ck, sk = ck_ref[...], sk_ref[...]
    qr = jnp.concatenate([q1 * cq - q2 * sq, q2 * cq + q1 * sq], axis=-1)
    kr = jnp.concatenate([k1 * ck - k2 * sk, k2 * ck + k1 * sk], axis=-1)
    qb = qr.astype(BF16)
    kb = kr.astype(BF16)
    vb = v_ref[...].astype(BF16)
    s_old = s_ref[0, 0]

    scores = lax.dot_general(qb, kb, (((1,), (1,)), ((), ())),
                             preferred_element_type=F32) * dm_ref[0]
    intra = jnp.dot(scores.astype(BF16), vb, preferred_element_type=F32)
    cross = jnp.dot(qb, s_old.astype(BF16), preferred_element_type=F32) * qd_ref[0]
    o = intra + cross

    kd = (kr * kd_ref[0]).astype(BF16)
    s_ref[0, 0] = gc_ref[0] * s_old + lax.dot_general(
        kd, vb, (((0,), (0,)), ((), ())), preferred_element_type=F32)

    on = o * lax.rsqrt(jnp.mean(o * o, axis=-1, keepdims=True) + EPS)
    gt = gt_ref[...]
    og_ref[...] = (gt * jax.nn.sigmoid(gt) * on).astype(og_ref.dtype)


def _retention_tables(chunk, pos, valid):
    dk = 256
    half = dk // 2
    lg = jnp.log1p(-jnp.exp2(-5.0 - jnp.arange(RET_HEADS, dtype=F32)))
    idx = jnp.arange(chunk, dtype=F32)
    diff = idx[:, None] - idx[None, :]
    dmat = jnp.where(diff >= 0, jnp.exp(lg[:, None, None] * jnp.maximum(diff, 0.0)), 0.0)
    qdec = jnp.exp(lg[:, None] * (idx[None, :] + 1.0))[:, :, None]
    kdec = jnp.exp(lg[:, None] * (chunk - 1.0 - idx[None, :]))[:, :, None]
    gc = jnp.exp(lg * chunk)[:, None, None]
    inv_freq = ROPE_BASE ** (-jnp.arange(half, dtype=F32) / half)
    ang = pos.astype(F32)[:, None] * inv_freq[None, :]
    cos, sin = jnp.cos(ang), jnp.sin(ang)
    kscale = (dk ** -0.5) * valid.astype(F32)[:, None]
    return dmat, qdec, kdec, gc, cos, sin, cos * kscale, sin * kscale


def retention(p, s0, chunk, pos, valid, out_dtype):
    b, h, dk, dv = s0.shape
    n = p.shape[0]
    t = n // b
    nc = t // chunk
    dmat, qdec, kdec, gc, cq, sq, ck, sk = _retention_tables(chunk, pos, valid)
    row = lambda bi, hi, ci: bi * nc + ci
    qoff = 0
    koff = (h * dk) // dk
    voff = (2 * h * dk) // dv
    goff = (2 * h * dk + h * dv) // dv
    rope_spec = pl.BlockSpec((chunk, dk // 2), lambda bi, hi, ci: (ci, 0))
    return pl.pallas_call(
        _retention_kernel,
        grid=(b, h, nc),
        in_specs=[
            pl.BlockSpec((chunk, dk), lambda bi, hi, ci: (row(bi, hi, ci), qoff + hi)),
            pl.BlockSpec((chunk, dk), lambda bi, hi, ci: (row(bi, hi, ci), koff + hi)),
            pl.BlockSpec((chunk, dv), lambda bi, hi, ci: (row(bi, hi, ci), voff + hi)),
            pl.BlockSpec((chunk, dv), lambda bi, hi, ci: (row(bi, hi, ci), goff + hi)),
            rope_spec, rope_spec, rope_spec, rope_spec,
            pl.BlockSpec((1, chunk, chunk), lambda bi, hi, ci: (hi, 0, 0)),
            pl.BlockSpec((1, chunk, 1), lambda bi, hi, ci: (hi, 0, 0)),
            pl.BlockSpec((1, chunk, 1), lambda bi, hi, ci: (hi, 0, 0)),
            pl.BlockSpec((1, 1, 1), lambda bi, hi, ci: (hi, 0, 0)),
            pl.BlockSpec((1, 1, dk, dv), lambda bi, hi, ci: (bi, hi, 0, 0)),
        ],
        out_specs=[
            pl.BlockSpec((chunk, dv), lambda bi, hi, ci: (row(bi, hi, ci), hi)),
            pl.BlockSpec((1, 1, dk, dv), lambda bi, hi, ci: (bi, hi, 0, 0)),
        ],
        out_shape=[
            jax.ShapeDtypeStruct((n, h * dv), out_dtype),
            jax.ShapeDtypeStruct((b, h, dk, dv), F32),
        ],
        compiler_params=_cparams(("parallel", "parallel", "arbitrary")),
        name="retention",
    )(p, p, p, p, cq, sq, ck, sk, dmat, qdec, kdec, gc, s0)


def _proj_mlp_kernel(x_ref, a_ref, wo_ref, g_ref, wu_ref, wd_ref, o_ref, xn_ref):
    @pl.when(pl.program_id(1) == 0)
    def _():
        x1 = x_ref[...] + jnp.dot(a_ref[...].astype(BF16), wo_ref[...], preferred_element_type=F32)
        o_ref[...] = x1
        xn_ref[...] = _rms(x1, g_ref[...]).astype(BF16)

    hdn = jnp.dot(xn_ref[...], wu_ref[...], preferred_element_type=F32)
    hdn = jnp.square(jnp.maximum(hdn, 0.0)).astype(BF16)
    o_ref[...] += jnp.dot(hdn, wd_ref[...], preferred_element_type=F32)


def proj_mlp(x, a, wo, g, wu, wd):
    n, d = x.shape
    ka = a.shape[1]
    dff = wu.shape[1]
    tm = _tile(n, 768, 16)
    tf = _tile(dff, 512, LANES)
    return pl.pallas_call(
        _proj_mlp_kernel,
        grid=(n // tm, dff // tf),
        in_specs=[
            pl.BlockSpec((tm, d), lambda i, f: (i, 0)),
            pl.BlockSpec((tm, ka), lambda i, f: (i, 0)),
            pl.BlockSpec((ka, d), lambda i, f: (0, 0)),
            pl.BlockSpec((1, d), lambda i, f: (0, 0)),
            pl.BlockSpec((d, tf), lambda i, f: (0, f)),
            pl.BlockSpec((tf, d), lambda i, f: (f, 0)),
        ],
        out_specs=pl.BlockSpec((tm, d), lambda i, f: (i, 0)),
        out_shape=jax.ShapeDtypeStruct((n, d), F32),
        scratch_shapes=[pltpu.VMEM((tm, d), BF16)],
        compiler_params=_cparams(("parallel", "arbitrary")),
        name="proj_mlp",
    )(x, a, wo, g.reshape(1, d), wu, wd)


def _kv_kernel(x_ref, g_ref, wk_ref, wv_ref, wf_ref, bf_ref, gk_ref, bd_ref, tri_ref,
               k_ref, v_ref, lf_ref, cum_ref, ckm_ref, kb_ref, vb_ref, carry_ref, *, pad):
    t = pl.program_id(1)
    tm = x_ref.shape[0]

    @pl.when(t == 0)
    def _():
        carry_ref[...] = jnp.zeros_like(carry_ref)

    xn = _rms(x_ref[...], g_ref[...]).astype(BF16)
    kraw = jnp.dot(xn, wk_ref[...], preferred_element_type=F32)
    k = _head_rms(kraw, bd_ref, gk_ref[...])
    v = jnp.dot(xn, wv_ref[...], preferred_element_type=F32)
    k_ref[...] = k
    v_ref[...] = v
    kb_ref[...] = k.astype(BF16)
    vb_ref[...] = v.astype(BF16)

    z = jnp.dot(xn, wf_ref[...], preferred_element_type=F32) + bf_ref[...]
    logf = jnp.minimum(z, 0.0) - jnp.log1p(jnp.exp(-jnp.abs(z)))
    rows = t * tm + lax.broadcasted_iota(jnp.int32, (tm, LANES), 0)
    valid = rows >= pad
    logf = jnp.where(valid, logf, 0.0)
    hi, mid, lo = _split3(logf)
    tri = tri_ref[...]
    cum = carry_ref[...] + (jnp.dot(tri, hi, preferred_element_type=F32)
                            + jnp.dot(tri, mid, preferred_element_type=F32)
                            + jnp.dot(tri, lo, preferred_element_type=F32))
    carry_ref[...] = cum[tm - 1:tm, :]
    nh = lf_ref.shape[1]
    lf_ref[...] = logf[:, :nh]
    cum_ref[...] = cum[:, :nh]
    ckm_ref[...] = jnp.where(valid, cum, -MASK_VALUE)[:, :nh]


def kv_proj(x, g, wk, wv, wf, bf, gk_t, bd, tri, nb, pad):
    n, d = x.shape
    tm = tri.shape[0]
    tb = n // nb
    nt = max(tb // tm, 1)
    nbg = n // (tm * nt)
    nh = FOX_HEADS
    row = lambda bi, ti: (bi * nt + ti, 0)
    full = lambda bi, ti: (0, 0)
    return pl.pallas_call(
        functools.partial(_kv_kernel, pad=pad),
        grid=(nbg, nt),
        in_specs=[
            pl.BlockSpec((tm, d), row),
            pl.BlockSpec((1, d), full),
            pl.BlockSpec((d, d), full),
            pl.BlockSpec((d, d), full),
            pl.BlockSpec((d, LANES), full),
            pl.BlockSpec((1, LANES), full),
            pl.BlockSpec((1, d), full),
            pl.BlockSpec((MXU_DIM, MXU_DIM), full),
            pl.BlockSpec((tm, tm), full),
        ],
        out_specs=[
            pl.BlockSpec((tm, d), row),
            pl.BlockSpec((tm, d), row),
            pl.BlockSpec((tm, nh), row),
            pl.BlockSpec((tm, nh), row),
            pl.BlockSpec((tm, nh), row),
            pl.BlockSpec((tm, d), row),
            pl.BlockSpec((tm, d), row),
        ],
        out_shape=[
            jax.ShapeDtypeStruct((n, d), F32),
            jax.ShapeDtypeStruct((n, d), F32),
            jax.ShapeDtypeStruct((n, nh), F32),
            jax.ShapeDtypeStruct((n, nh), F32),
            jax.ShapeDtypeStruct((n, nh), F32),
            jax.ShapeDtypeStruct((n, d), BF16),
            jax.ShapeDtypeStruct((n, d), BF16),
        ],
        scratch_shapes=[pltpu.VMEM((1, LANES), F32)],
        compiler_params=_cparams(("parallel", "arbitrary")),
        name="kv_proj",
    )(x, g.reshape(1, d), wk, wv, wf, bf, gk_t, bd, tri)


def _fox_in_kernel(x_ref, g_ref, wq_ref, wg_ref, gq_ref, bd_ref, q_ref, gate_ref):
    xn = _rms(x_ref[...], g_ref[...]).astype(BF16)
    qraw = jnp.dot(xn, wq_ref[...], preferred_element_type=F32)
    q_ref[...] = _head_rms(qraw, bd_ref, gq_ref[...]).astype(BF16)
    gate_ref[...] = jnp.dot(xn, wg_ref[...], preferred_element_type=F32)


def fox_in(x, g, wq, wg, gq_t, bd):
    n, d = x.shape
    tm = _tile(n, 1024, 16)
    row = lambda i: (i, 0)
    full = lambda i: (0, 0)
    return pl.pallas_call(
        _fox_in_kernel,
        grid=(n // tm,),
        in_specs=[
            pl.BlockSpec((tm, d), row),
            pl.BlockSpec((1, d), full),
            pl.BlockSpec((d, d), full),
            pl.BlockSpec((d, d), full),
            pl.BlockSpec((1, d), full),
            pl.BlockSpec((MXU_DIM, MXU_DIM), full),
        ],
        out_specs=[pl.BlockSpec((tm, d), row), pl.BlockSpec((tm, d), row)],
        out_shape=[jax.ShapeDtypeStruct((n, d), BF16), jax.ShapeDtypeStruct((n, d), F32)],
        compiler_params=_cparams(("parallel",)),
        name="fox_in",
    )(x, g.reshape(1, d), wq, wg, gq_t, bd)


def _fox_prompt_kernel(q_ref, k_ref, v_ref, g_ref, ck_ref, o_ref, m_ref, l_ref, acc_ref, *, tk, pre):
    i = pl.program_id(2)
    tq = q_ref.shape[1]
    q = q_ref[0]
    lane = lax.broadcasted_iota(jnp.int32, (tq, LANES), 1)
    low = lane < FOX_DH
    zero = jnp.zeros_like(q)
    qs = jnp.concatenate([jnp.where(low, q, zero), jnp.where(low, zero, q)], axis=0)

    m_ref[...] = jnp.full_like(m_ref, MASK_VALUE)
    l_ref[...] = jnp.zeros_like(l_ref)
    acc_ref[...] = jnp.zeros_like(acc_ref)

    def block(j0, width, masked):
        kj = k_ref[0, pl.ds(j0, width), :]
        vj = v_ref[0, pl.ds(j0, width), :]
        s = lax.dot_general(qs, kj, (((1,), (1,)), ((), ())), preferred_element_type=F32)
        ck = ck_ref[0, 0, :, pl.ds(j0, width)]
        s0 = s[:tq] - ck[0:1, :]
        s1 = s[tq:] - ck[1:2, :]
        if masked:
            r = i * tq + lax.broadcasted_iota(jnp.int32, (tq, width), 0)
            c = (j0 - pre) + lax.broadcasted_iota(jnp.int32, (tq, width), 1)
            ok = c <= r
            s0 = jnp.where(ok, s0, MASK_VALUE)
            s1 = jnp.where(ok, s1, MASK_VALUE)
        s = jnp.concatenate([s0, s1], axis=0)
        m_old = m_ref[...]
        m_new = jnp.maximum(m_old, jnp.max(s, axis=-1, keepdims=True))
        p = jnp.exp(s - m_new)
        alpha = jnp.exp(m_old - m_new)
        l_ref[...] = alpha * l_ref[...] + jnp.sum(p, axis=-1, keepdims=True)
        acc_ref[...] = alpha * acc_ref[...] + jnp.dot(p.astype(BF16), vj, preferred_element_type=F32)
        m_ref[...] = m_new

    if pre:
        block(0, pre, False)
    nfull = i * (tq // tk)

    def body(j, carry):
        block(pl.multiple_of(pre + j * tk, LANES), tk, False)
        return carry

    lax.fori_loop(0, nfull, body, 0)
    for dblk in range(tq // tk):
        block(pl.multiple_of(pre + i * tq + dblk * tk, LANES), tk, True)

    o = acc_ref[...] / l_ref[...]
    o = jnp.where(low, o[:tq], o[tq:])
    o_ref[0] = (o * jax.nn.sigmoid(g_ref[0])).astype(o_ref.dtype)


def fox_prompt_attention(q, kb, vb, gate, ck_t, pre):
    b, tq_all, d = q.shape
    tk_all = kb.shape[1]
    hp = d // LANES
    tq = _tile(tq_all, 512, LANES)
    tk = _tile(tq, 256, LANES)
    return pl.pallas_call(
        functools.partial(_fox_prompt_kernel, tk=tk, pre=pre),
        grid=(b, hp, tq_all // tq),
        in_specs=[
            pl.BlockSpec((1, tq, LANES), lambda bi, hi, qi: (bi, qi, hi)),
            pl.BlockSpec((1, tk_all, LANES), lambda bi, hi, qi: (bi, 0, hi)),
            pl.BlockSpec((1, tk_all, LANES), lambda bi, hi, qi: (bi, 0, hi)),
            pl.BlockSpec((1, tq, LANES), lambda bi, hi, qi: (bi, qi, hi)),
            pl.BlockSpec((1, 1, 2, tk_all), lambda bi, hi, qi: (bi, hi, 0, 0)),
        ],
        out_specs=pl.BlockSpec((1, tq, LANES), lambda bi, hi, qi: (bi, qi, hi)),
        out_shape=jax.ShapeDtypeStruct((b, tq_all, d), BF16),
        scratch_shapes=[
            pltpu.VMEM((2 * tq, 1), F32),
            pltpu.VMEM((2 * tq, 1), F32),
            pltpu.VMEM((2 * tq, LANES), F32),
        ],
        compiler_params=_cparams(("parallel", "parallel", "arbitrary")),
        name="fox_prompt_attention",
    )(q, kb, vb, gate, ck_t)


def _fox_decode_kernel(pt_ref, q_ref, g_ref, kn_ref, vn_ref, cn_ref, tri_ref, *rest, pg, ds):
    k_refs = rest[:pg]
    v_refs = rest[pg:2 * pg]
    lf_refs = rest[2 * pg:3 * pg]
    o_ref = rest[3 * pg]
    qbd_ref, ks_ref, vs_ref, m_ref, l_ref, acc_ref, carry_ref = rest[3 * pg + 1:]
    j = pl.program_id(1)
    nh = FOX_HEADS
    nr = ds * nh
    d = q_ref.shape[2]
    page = k_refs[0].shape[1]

    def head_mask(shape, row_axis, lane_axis):
        hrow = lax.broadcasted_iota(jnp.int32, shape, row_axis) & (nh - 1)
        hlane = lax.broadcasted_iota(jnp.int32, shape, lane_axis) >> 6
        return hrow == hlane

    @pl.when(j == 0)
    def _():
        q = q_ref[0].astype(F32)
        qrep = jnp.concatenate(
            [jnp.broadcast_to(q[qi:qi + 1, :], (nh, d)) for qi in range(ds)], axis=0)
        qbd_ref[...] = jnp.where(head_mask((nr, d), 0, 1), qrep, 0.0).astype(BF16)
        m_ref[...] = jnp.full_like(m_ref, MASK_VALUE)
        l_ref[...] = jnp.zeros_like(l_ref)
        acc_ref[...] = jnp.zeros_like(acc_ref)
        carry_ref[...] = jnp.zeros_like(carry_ref)

    def update(s, v_bf):
        m_old = m_ref[...]
        m_new = jnp.maximum(m_old, jnp.max(s, axis=-1, keepdims=True))
        p = jnp.exp(s - m_new)
        alpha = jnp.exp(m_old - m_new)
        l_ref[...] = alpha * l_ref[...] + jnp.sum(p, axis=-1, keepdims=True)
        acc_ref[...] = alpha * acc_ref[...] + jnp.dot(p.astype(BF16), v_bf, preferred_element_type=F32)
        m_ref[...] = m_new

    parts = []
    for pi in range(pg):
        ks_ref[pi * page:(pi + 1) * page, :] = k_refs[pi][0].astype(BF16)
        vs_ref[pi * page:(pi + 1) * page, :] = v_refs[pi][0].astype(BF16)
        parts.extend(_split3(lf_refs[pi][0]))
    w = jnp.dot(jnp.concatenate(parts, axis=0), tri_ref[...], preferred_element_type=F32)
    carry = carry_ref[...]
    cums = []
    for pi in range(pg):
        base = 3 * pi * nh
        wp = w[base:base + nh] + w[base + nh:base + 2 * nh] + w[base + 2 * nh:base + 3 * nh]
        cp = carry + wp
        cums.append(cp)
        carry = cp[:, page - 1:page]
    carry_ref[...] = carry
    ck = jnp.concatenate(cums, axis=-1)
    bias = jnp.concatenate([ck] * ds, axis=0)

    s = lax.dot_general(qbd_ref[...], ks_ref[...], (((1,), (1,)), ((), ())),
                        preferred_element_type=F32) - bias
    update(s, vs_ref[...])

    @pl.when(j == pl.num_programs(1) - 1)
    def _():
        kn = kn_ref[0]
        vn = vn_ref[0]
        cn = carry_ref[...] + cn_ref[0]
        sn = lax.dot_general(qbd_ref[...], kn, (((1,), (1,)), ((), ())),
                             preferred_element_type=F32) - jnp.concatenate([cn] * ds, axis=0)
        qi = lax.broadcasted_iota(jnp.int32, (nr, page), 0) >> 4
        kj = lax.broadcasted_iota(jnp.int32, (nr, page), 1)
        sn = jnp.where(kj <= qi, sn, MASK_VALUE)
        update(sn, vn)
        acc = acc_ref[...] / l_ref[...]
        acc = jnp.where(head_mask((nr, d), 0, 1), acc, 0.0)
        o = jnp.concatenate(
            [jnp.sum(acc[qi_ * nh:(qi_ + 1) * nh, :], axis=0, keepdims=True) for qi_ in range(ds)], axis=0)
        o_ref[0] = (o * jax.nn.sigmoid(g_ref[0])).astype(o_ref.dtype)


def fox_decode_attention(page_table, q, gate, k_new, v_new, cum_new_t, tri_u, cache_k, cache_v, cache_lf_t):
    db, ds, d = q.shape
    npages = page_table.shape[1]
    page = cache_k.shape[1]
    pg = _tile(npages, PAGES_PER_STEP, 1)
    nh = FOX_HEADS
    nr = ds * nh

    def page_spec(shape, pi):
        return pl.BlockSpec(shape, lambda bi, ji, pt: (pt[bi, ji * pg + pi], 0, 0))

    per_b3 = lambda bi, ji, pt: (bi, 0, 0)
    in_specs = [
        pl.BlockSpec((1, ds, d), per_b3),
        pl.BlockSpec((1, ds, d), per_b3),
        pl.BlockSpec((1, page, d), per_b3),
        pl.BlockSpec((1, page, d), per_b3),
        pl.BlockSpec((1, nh, page), per_b3),
        pl.BlockSpec((page, page), lambda bi, ji, pt: (0, 0)),
    ]
    in_specs += [page_spec((1, page, d), pi) for pi in range(pg)]
    in_specs += [page_spec((1, page, d), pi) for pi in range(pg)]
    in_specs += [page_spec((1, nh, page), pi) for pi in range(pg)]
    grid_spec = pltpu.PrefetchScalarGridSpec(
        num_scalar_prefetch=1,
        grid=(db, npages // pg),
        in_specs=in_specs,
        out_specs=pl.BlockSpec((1, ds, d), per_b3),
        scratch_shapes=[
            pltpu.VMEM((nr, d), BF16),
            pltpu.VMEM((pg * page, d), BF16),
            pltpu.VMEM((pg * page, d), BF16),
            pltpu.VMEM((nr, 1), F32),
            pltpu.VMEM((nr, 1), F32),
            pltpu.VMEM((nr, d), F32),
            pltpu.VMEM((nh, 1), F32),
        ],
    )
    return pl.pallas_call(
        functools.partial(_fox_decode_kernel, pg=pg, ds=ds),
        grid_spec=grid_spec,
        out_shape=jax.ShapeDtypeStruct((db, ds, d), F32),
        compiler_params=_cparams(("parallel", "arbitrary")),
        name="fox_decode_attention",
    )(page_table, q, gate, k_new, v_new, cum_new_t, tri_u,
      *([cache_k] * pg), *([cache_v] * pg), *([cache_lf_t] * pg))


def _block_diag_ones(n, blk):
    i = jnp.arange(n)
    return (i[:, None] // blk == i[None, :] // blk).astype(BF16)


def _lower_tri(n, seq):
    i = jnp.arange(n)
    return ((i[:, None] >= i[None, :]) & (i[:, None] // seq == i[None, :] // seq)).astype(BF16)


def _prep_weights(w_ret_in, w_ret_out, w_kvf, b_f, g_k, w_fox_qg, g_q, w_fox_out, w_mlp_up, w_mlp_down):
    d = w_kvf.shape[0]
    nh = FOX_HEADS
    wf = jnp.zeros((d, LANES), BF16).at[:, :nh].set(w_kvf[:, 2 * d:].astype(BF16))
    bf = jnp.zeros((1, LANES), F32).at[0, :nh].set(b_f)
    return dict(
        ret_in=w_ret_in.astype(BF16), ret_out=w_ret_out.astype(BF16),
        wk=w_kvf[:, :d].astype(BF16), wv=w_kvf[:, d:2 * d].astype(BF16), wf=wf, bf=bf,
        gk_t=jnp.tile(g_k, nh).reshape(1, d),
        wq=w_fox_qg[:, :, :d].astype(BF16), wg=w_fox_qg[:, :, d:].astype(BF16),
        gq_t=(jnp.tile(g_q, (1, nh)) * (FOX_DH ** -0.5)).reshape(-1, 1, d),
        fox_out=w_fox_out.astype(BF16), up=w_mlp_up.astype(BF16), down=w_mlp_down.astype(BF16),
        bd=_block_diag_ones(MXU_DIM, FOX_DH),
    )


def _prompt_forward(x_prompt, meta, g_attn, g_mlp, w):
    b, seq, d = x_prompt.shape
    pad = RET_CHUNK - N_META
    x = jnp.concatenate([jnp.zeros((b, pad, d), F32),
                         jnp.broadcast_to(meta[None], (b, N_META, d)), x_prompt], axis=1)
    t = x.shape[1]
    pos = jnp.arange(t) - pad
    valid = pos >= 0
    x = x.reshape(b * t, d)
    n_ret = w["ret_in"].shape[0]
    states = []
    for l in range(n_ret):
        p = norm_matmul(x, g_attn[l], w["ret_in"][l])
        s0 = jnp.zeros((b, RET_HEADS, 256, 512), F32)
        og, s_new = retention(p, s0, RET_CHUNK, pos, valid, BF16)
        states.append(s_new)
        x = proj_mlp(x, og, w["ret_out"][l], g_mlp[l], w["up"][l], w["down"][l])

    tm = _tile(t, 512, LANES)
    k, v, logf, _, ckm, kb, vb = kv_proj(x, w["g_kv"], w["wk"], w["wv"], w["wf"], w["bf"], w["gk_t"],
                                         w["bd"], _lower_tri(tm, tm), b, pad)
    pre = RET_CHUNK
    nh = FOX_HEADS
    ck_t = ckm.reshape(b, t, nh // 2, 2).transpose(0, 2, 3, 1)
    kb3, vb3 = kb.reshape(b, t, d), vb.reshape(b, t, d)
    xr = x.reshape(b, t, d)[:, pre:].reshape(b * seq, d)
    for l in range(n_ret, g_attn.shape[0]):
        q, gate = fox_in(xr, g_attn[l], w["wq"][l - n_ret], w["wg"][l - n_ret], w["gq_t"][l - n_ret], w["bd"])
        a = fox_prompt_attention(q.reshape(b, seq, d), kb3, vb3, gate.reshape(b, seq, d), ck_t, pre)
        xr = proj_mlp(xr, a.reshape(b * seq, d), w["fox_out"][l - n_ret], g_mlp[l], w["up"][l], w["down"][l])
    y = xr.reshape(b, seq, d)
    k4 = k.reshape(b, t, nh, FOX_DH)[:, pad:]
    v4 = v.reshape(b, t, nh, FOX_DH)[:, pad:]
    return y, jnp.stack(states), k4, v4, logf.reshape(b, t, nh)[:, pad:]


def _sample_forward(x_sample, state_ret, cache_k, cache_v, cache_logf, page_table, g_attn, g_mlp, w):
    db, ds, d = x_sample.shape
    n_pool, page, nh, dh = cache_k.shape
    past = page_table.shape[1] * page
    pos = past + jnp.arange(ds)
    valid = jnp.ones((ds,), bool)
    x = x_sample.reshape(db * ds, d)
    n_ret = w["ret_in"].shape[0]
    states = []
    for l in range(n_ret):
        p = norm_matmul(x, g_attn[l], w["ret_in"][l])
        og, s_new = retention(p, state_ret[l], ds, pos, valid, F32)
        states.append(s_new)
        x = proj_mlp(x, og, w["ret_out"][l], g_mlp[l], w["up"][l], w["down"][l])

    n = db * ds
    k, v, logf, cum, _, kb, vb = kv_proj(x, w["g_kv"], w["wk"], w["wv"], w["wf"], w["bf"], w["gk_t"],
                                         w["bd"], _lower_tri(n, ds), 1, 0)
    zrows = jnp.zeros((db, page - ds, d), BF16)
    k_new = jnp.concatenate([kb.reshape(db, ds, d), zrows], axis=1)
    v_new = jnp.concatenate([vb.reshape(db, ds, d), zrows], axis=1)
    cum_t = jnp.zeros((db, nh, page), F32).at[:, :, :ds].set(cum.reshape(db, ds, nh).transpose(0, 2, 1))
    tri_u = _lower_tri(page, page).T
    ck3 = cache_k.reshape(n_pool, page, d)
    cv3 = cache_v.reshape(n_pool, page, d)
    clf_t = cache_logf.transpose(0, 2, 1)
    for l in range(n_ret, g_attn.shape[0]):
        q, gate = fox_in(x, g_attn[l], w["wq"][l - n_ret], w["wg"][l - n_ret], w["gq_t"][l - n_ret], w["bd"])
        a = fox_decode_attention(page_table, q.reshape(db, ds, d), gate.reshape(db, ds, d),
                                 k_new, v_new, cum_t, tri_u, ck3, cv3, clf_t)
        x = proj_mlp(x, a.reshape(n, d), w["fox_out"][l - n_ret], g_mlp[l], w["up"][l], w["down"][l])
    return (x.reshape(db, ds, d), jnp.stack(states), k.reshape(db, ds, nh, dh), v.reshape(db, ds, nh, dh),
            logf.reshape(db, ds, nh))


def kernel(x_prompt, x_sample, state_ret, cache_k, cache_v, cache_logf, page_table, meta, g_attn, g_mlp,
           w_ret_in, w_ret_out, g_kv, w_kvf, b_f, g_k, w_fox_qg, g_q, w_fox_out, w_mlp_up, w_mlp_down):
    w = _prep_weights(w_ret_in, w_ret_out, w_kvf, b_f, g_k, w_fox_qg, g_q, w_fox_out, w_mlp_up, w_mlp_down)
    w["g_kv"] = g_kv
    y_p, s_p, k_p, v_p, lf_p = _prompt_forward(x_prompt, meta, g_attn, g_mlp, w)
    y_s, s_s, k_s, v_s, lf_s = _sample_forward(x_sample, state_ret, cache_k, cache_v, cache_logf, page_table,
                                               g_attn, g_mlp, w)
    return (y_p, y_s, s_p, s_s, k_p, v_p, lf_p, k_s, v_s, lf_s)
```

```python
import functools

import jax
import jax.numpy as jnp
from jax import lax
from jax.experimental import pallas as pl
from jax.experimental.pallas import tpu as pltpu

F32 = jnp.float32
BF16 = jnp.bfloat16

N_META = 16
RET_HEADS = 4
RET_CHUNK = 128
ROPE_BASE = 10000.0
FOX_HEADS = 16
FOX_DH = 64
EPS = 1e-6
MASK_VALUE = -1e30

LANES = 128
MXU_DIM = 256
VMEM_LIMIT = 56 * 1024 * 1024
PAGES_PER_STEP = 8


def _cparams(sem):
    return pltpu.CompilerParams(dimension_semantics=sem, vmem_limit_bytes=VMEM_LIMIT)


def _tile(n, target, mult=8):
    best = None
    for t in range(mult, min(n, target) + 1, mult):
        if n % t == 0:
            best = t
    assert best is not None, (n, target, mult)
    return best


def _rms(x, g_row):
    ms = jnp.mean(x * x, axis=-1, keepdims=True)
    return x * lax.rsqrt(ms + EPS) * g_row


def _split2(x):
    hi = x.astype(BF16)
    lo = (x - hi.astype(F32)).astype(BF16)
    return hi, lo


def _split3(x):
    hi = x.astype(BF16)
    r = x - hi.astype(F32)
    mid = r.astype(BF16)
    lo = (r - mid.astype(F32)).astype(BF16)
    return hi, mid, lo


def _head_rms(x, bd_ref, g_row):
    xx = x * x
    hi, lo = _split2(xx)
    bd = bd_ref[...]
    parts = []
    for c in range(x.shape[1] // MXU_DIM):
        sl = slice(c * MXU_DIM, (c + 1) * MXU_DIM)
        parts.append(jnp.dot(hi[:, sl], bd, preferred_element_type=F32)
                     + jnp.dot(lo[:, sl], bd, preferred_element_type=F32))
    ss = jnp.concatenate(parts, axis=-1)
    return x * lax.rsqrt(ss * (1.0 / FOX_DH) + EPS) * g_row


def _norm_matmul_kernel(x_ref, g_ref, w_ref, o_ref, xn_ref):
    @pl.when(pl.program_id(1) == 0)
    def _():
        xn_ref[...] = _rms(x_ref[...], g_ref[...]).astype(BF16)

    o_ref[...] = jnp.dot(xn_ref[...], w_ref[...], preferred_element_type=F32)


def norm_matmul(x, g, w):
    n, d = x.shape
    nout = w.shape[1]
    tm = _tile(n, 1536)
    tn = _tile(nout, 1024, LANES)
    return pl.pallas_call(
        _norm_matmul_kernel,
        grid=(n // tm, nout // tn),
        in_specs=[
            pl.BlockSpec((tm, d), lambda i, j: (i, 0)),
            pl.BlockSpec((1, d), lambda i, j: (0, 0)),
            pl.BlockSpec((d, tn), lambda i, j: (0, j)),
        ],
        out_specs=pl.BlockSpec((tm, tn), lambda i, j: (i, j)),
        out_shape=jax.ShapeDtypeStruct((n, nout), F32),
        scratch_shapes=[pltpu.VMEM((tm, d), BF16)],
        compiler_params=_cparams(("parallel", "arbitrary")),
        name="norm_matmul",
    )(x, g.reshape(1, d), w)


def _retention_kernel(q_ref, k_ref, v_ref, gt_ref, cq_ref, sq_ref, ck_ref, sk_ref,
                      dm_ref, qd_ref, kd_ref, gc_ref, s0_ref, og_ref, s_ref):
    @pl.when(pl.program_id(2) == 0)
    def _():
        s_ref[...] = s0_ref[...]

    half = q_ref.shape[1] // 2
    q = q_ref[...]
    k = k_ref[...]
    q1, q2 = q[:, :half], q[:, half:]
    k1, k2 = k[:, :half], k[:, half:]
    cq, sq = cq_ref[...], sq_ref[...]
    ck, sk = ck_ref[...], sk_ref[...]
    qr = jnp.concatenate([q1 * cq - q2 * sq, q2 * cq + q1 * sq], axis=-1)
    kr = jnp.concatenate([k1 * ck - k2 * sk, k2 * ck + k1 * sk], axis=-1)
    qb = qr.astype(BF16)
    kb = kr.astype(BF16)
    vb = v_ref[...].astype(BF16)
    s_old = s_ref[0, 0]

    scores = lax.dot_general(qb, kb, (((1,), (1,)), ((), ())),
                             preferred_element_type=F32) * dm_ref[0]
    intra = jnp.dot(scores.astype(BF16), vb, preferred_element_type=F32)
    cross = jnp.dot(qb, s_old.astype(BF16), preferred_element_type=F32) * qd_ref[0]
    o = intra + cross

    kd = (kr * kd_ref[0]).astype(BF16)
    s_ref[0, 0] = gc_ref[0] * s_old + lax.dot_general(
        kd, vb, (((0,), (0,)), ((), ())), preferred_element_type=F32)

    on = o * lax.rsqrt(jnp.mean(o * o, axis=-1, keepdims=True) + EPS)
    gt = gt_ref[...]
    og_ref[...] = (gt * jax.nn.sigmoid(gt) * on).astype(og_ref.dtype)


def _retention_tables(chunk, pos, valid):
    dk = 256
    half = dk // 2
    lg = jnp.log1p(-jnp.exp2(-5.0 - jnp.arange(RET_HEADS, dtype=F32)))
    idx = jnp.arange(chunk, dtype=F32)
    diff = idx[:, None] - idx[None, :]
    dmat = jnp.where(diff >= 0, jnp.exp(lg[:, None, None] * jnp.maximum(diff, 0.0)), 0.0)
    qdec = jnp.exp(lg[:, None] * (idx[None, :] + 1.0))[:, :, None]
    kdec = jnp.exp(lg[:, None] * (chunk - 1.0 - idx[None, :]))[:, :, None]
    gc = jnp.exp(lg * chunk)[:, None, None]
    inv_freq = ROPE_BASE ** (-jnp.arange(half, dtype=F32) / half)
    ang = pos.astype(F32)[:, None] * inv_freq[None, :]
    cos, sin = jnp.cos(ang), jnp.sin(ang)
    kscale = (dk ** -0.5) * valid.astype(F32)[:, None]
    return dmat, qdec, kdec, gc, cos, sin, cos * kscale, sin * kscale


def retention(p, s0, chunk, pos, valid, out_dtype):
    b, h, dk, dv = s0.shape
    n = p.shape[0]
    t = n // b
    nc = t // chunk
    dmat, qdec, kdec, gc, cq, sq, ck, sk = _retention_tables(chunk, pos, valid)
    row = lambda bi, hi, ci: bi * nc + ci
    qoff = 0
    koff = (h * dk) // dk
    voff = (2 * h * dk) // dv
    goff = (2 * h * dk + h * dv) // dv
    rope_spec = pl.BlockSpec((chunk, dk // 2), lambda bi, hi, ci: (ci, 0))
    return pl.pallas_call(
        _retention_kernel,
        grid=(b, h, nc),
        in_specs=[
            pl.BlockSpec((chunk, dk), lambda bi, hi, ci: (row(bi, hi, ci), qoff + hi)),
            pl.BlockSpec((chunk, dk), lambda bi, hi, ci: (row(bi, hi, ci), koff + hi)),
            pl.BlockSpec((chunk, dv), lambda bi, hi, ci: (row(bi, hi, ci), voff + hi)),
            pl.BlockSpec((chunk, dv), lambda bi, hi, ci: (row(bi, hi, ci), goff + hi)),
            rope_spec, rope_spec, rope_spec, rope_spec,
            pl.BlockSpec((1, chunk, chunk), lambda bi, hi, ci: (hi, 0, 0)),
            pl.BlockSpec((1, chunk, 1), lambda bi, hi, ci: (hi, 0, 0)),
            pl.BlockSpec((1, chunk, 1), lambda bi, hi, ci: (hi, 0, 0)),
            pl.BlockSpec((1, 1, 1), lambda bi, hi, ci: (hi, 0, 0)),
            pl.BlockSpec((1, 1, dk, dv), lambda bi, hi, ci: (bi, hi, 0, 0)),
        ],
        out_specs=[
            pl.BlockSpec((chunk, dv), lambda bi, hi, ci: (row(bi, hi, ci), hi)),
            pl.BlockSpec((1, 1, dk, dv), lambda bi, hi, ci: (bi, hi, 0, 0)),
        ],
        out_shape=[
            jax.ShapeDtypeStruct((n, h * dv), out_dtype),
            jax.ShapeDtypeStruct((b, h, dk, dv), F32),
        ],
        compiler_params=_cparams(("parallel", "parallel", "arbitrary")),
        name="retention",
    )(p, p, p, p, cq, sq, ck, sk, dmat, qdec, kdec, gc, s0)


def _proj_mlp_kernel(x_ref, a_ref, wo_ref, g_ref, wu_ref, wd_ref, o_ref, xn_ref):
    @pl.when(pl.program_id(1) == 0)
    def _():
        x1 = x_ref[...] + jnp.dot(a_ref[...].astype(BF16), wo_ref[...], preferred_element_type=F32)
        o_ref[...] = x1
        xn_ref[...] = _rms(x1, g_ref[...]).astype(BF16)

    hdn = jnp.dot(xn_ref[...], wu_ref[...], preferred_element_type=F32)
    hdn = jnp.square(jnp.maximum(hdn, 0.0)).astype(BF16)
    o_ref[...] += jnp.dot(hdn, wd_ref[...], preferred_element_type=F32)


def proj_mlp(x, a, wo, g, wu, wd):
    n, d = x.shape
    ka = a.shape[1]
    dff = wu.shape[1]
    tm = _tile(n, 768, 16)
    tf = _tile(dff, 512, LANES)
    return pl.pallas_call(
        _proj_mlp_kernel,
        grid=(n // tm, dff // tf),
        in_specs=[
            pl.BlockSpec((tm, d), lambda i, f: (i, 0)),
            pl.BlockSpec((tm, ka), lambda i, f: (i, 0)),
            pl.BlockSpec((ka, d), lambda i, f: (0, 0)),
            pl.BlockSpec((1, d), lambda i, f: (0, 0)),
            pl.BlockSpec((d, tf), lambda i, f: (0, f)),
            pl.BlockSpec((tf, d), lambda i, f: (f, 0)),
        ],
        out_specs=pl.BlockSpec((tm, d), lambda i, f: (i, 0)),
        out_shape=jax.ShapeDtypeStruct((n, d), F32),
        scratch_shapes=[pltpu.VMEM((tm, d), BF16)],
        compiler_params=_cparams(("parallel", "arbitrary")),
        name="proj_mlp",
    )(x, a, wo, g.reshape(1, d), wu, wd)


def _kv_kernel(x_ref, g_ref, wk_ref, wv_ref, wf_ref, bf_ref, gk_ref, bd_ref, tri_ref,
               k_ref, v_ref, lf_ref, cum_ref, ckm_ref, kb_ref, vb_ref, carry_ref, *, pad):
    t = pl.program_id(1)
    tm = x_ref.shape[0]

    @pl.when(t == 0)
    def _():
        carry_ref[...] = jnp.zeros_like(carry_ref)

    xn = _rms(x_ref[...], g_ref[...]).astype(BF16)
    kraw = jnp.dot(xn, wk_ref[...], preferred_element_type=F32)
    k = _head_rms(kraw, bd_ref, gk_ref[...])
    v = jnp.dot(xn, wv_ref[...], preferred_element_type=F32)
    k_ref[...] = k
    v_ref[...] = v
    kb_ref[...] = k.astype(BF16)
    vb_ref[...] = v.astype(BF16)

    z = jnp.dot(xn, wf_ref[...], preferred_element_type=F32) + bf_ref[...]
    logf = jnp.minimum(z, 0.0) - jnp.log1p(jnp.exp(-jnp.abs(z)))
    rows = t * tm + lax.broadcasted_iota(jnp.int32, (tm, LANES), 0)
    valid = rows >= pad
    logf = jnp.where(valid, logf, 0.0)
    hi, mid, lo = _split3(logf)
    tri = tri_ref[...]
    cum = carry_ref[...] + (jnp.dot(tri, hi, preferred_element_type=F32)
                            + jnp.dot(tri, mid, preferred_element_type=F32)
                            + jnp.dot(tri, lo, preferred_element_type=F32))
    carry_ref[...] = cum[tm - 1:tm, :]
    nh = lf_ref.shape[1]
    lf_ref[...] = logf[:, :nh]
    cum_ref[...] = cum[:, :nh]
    ckm_ref[...] = jnp.where(valid, cum, -MASK_VALUE)[:, :nh]


def kv_proj(x, g, wk, wv, wf, bf, gk_t, bd, tri, nb, pad):
    n, d = x.shape
    tm = tri.shape[0]
    tb = n // nb
    nt = max(tb // tm, 1)
    nbg = n // (tm * nt)
    nh = FOX_HEADS
    row = lambda bi, ti: (bi * nt + ti, 0)
    full = lambda bi, ti: (0, 0)
    return pl.pallas_call(
        functools.partial(_kv_kernel, pad=pad),
        grid=(nbg, nt),
        in_specs=[
            pl.BlockSpec((tm, d), row),
            pl.BlockSpec((1, d), full),
            pl.BlockSpec((d, d), full),
            pl.BlockSpec((d, d), full),
            pl.BlockSpec((d, LANES), full),
            pl.BlockSpec((1, LANES), full),
            pl.BlockSpec((1, d), full),
            pl.BlockSpec((MXU_DIM, MXU_DIM), full),
            pl.BlockSpec((tm, tm), full),
        ],
        out_specs=[
            pl.BlockSpec((tm, d), row),
            pl.BlockSpec((tm, d), row),
            pl.BlockSpec((tm, nh), row),
            pl.BlockSpec((tm, nh), row),
            pl.BlockSpec((tm, nh), row),
            pl.BlockSpec((tm, d), row),
            pl.BlockSpec((tm, d), row),
        ],
        out_shape=[
            jax.ShapeDtypeStruct((n, d), F32),
            jax.ShapeDtypeStruct((n, d), F32),
            jax.ShapeDtypeStruct((n, nh), F32),
            jax.ShapeDtypeStruct((n, nh), F32),
            jax.ShapeDtypeStruct((n, nh), F32),
            jax.ShapeDtypeStruct((n, d), BF16),
            jax.ShapeDtypeStruct((n, d), BF16),
        ],
        scratch_shapes=[pltpu.VMEM((1, LANES), F32)],
        compiler_params=_cparams(("parallel", "arbitrary")),
        name="kv_proj",
    )(x, g.reshape(1, d), wk, wv, wf, bf, gk_t, bd, tri)


def _fox_in_kernel(x_ref, g_ref, wq_ref, wg_ref, gq_ref, bd_ref, q_ref, gate_ref):
    xn = _rms(x_ref[...], g_ref[...]).astype(BF16)
    qraw = jnp.dot(xn, wq_ref[...], preferred_element_type=F32)
    q_ref[...] = _head_rms(qraw, bd_ref, gq_ref[...]).astype(BF16)
    gate_ref[...] = jnp.dot(xn, wg_ref[...], preferred_element_type=F32)


def fox_in(x, g, wq, wg, gq_t, bd):
    n, d = x.shape
    tm = _tile(n, 1024, 16)
    row = lambda i: (i, 0)
    full = lambda i: (0, 0)
    return pl.pallas_call(
        _fox_in_kernel,
        grid=(n // tm,),
        in_specs=[
            pl.BlockSpec((tm, d), row),
            pl.BlockSpec((1, d), full),
            pl.BlockSpec((d, d), full),
            pl.BlockSpec((d, d), full),
            pl.BlockSpec((1, d), full),
            pl.BlockSpec((MXU_DIM, MXU_DIM), full),
        ],
        out_specs=[pl.BlockSpec((tm, d), row), pl.BlockSpec((tm, d), row)],
        out_shape=[jax.ShapeDtypeStruct((n, d), BF16), jax.ShapeDtypeStruct((n, d), F32)],
        compiler_params=_cparams(("parallel",)),
        name="fox_in",
    )(x, g.reshape(1, d), wq, wg, gq_t, bd)


def _fox_prompt_kernel(q_ref, k_ref, vt_ref, g_ref, ck_ref, o_ref, m_ref, l_ref, acc_ref, *, tk, pre):
    i = pl.program_id(2)
    tq = q_ref.shape[1]
    q = q_ref[0]
    low = lax.broadcasted_iota(jnp.int32, (tq, LANES), 1) < FOX_DH
    zero = jnp.zeros_like(q)
    qs = jnp.concatenate([jnp.where(low, q, zero), jnp.where(low, zero, q)], axis=0)

    m_ref[...] = jnp.full_like(m_ref, MASK_VALUE)
    l_ref[...] = jnp.zeros_like(l_ref)
    acc_ref[...] = jnp.zeros_like(acc_ref)

    def block(j0, width, masked):
        kj = k_ref[0, pl.ds(j0, width), :]
        vtj = vt_ref[0, 0, :, pl.ds(j0, width)]
        s = lax.dot_general(kj, qs, (((1,), (1,)), ((), ())), preferred_element_type=F32)
        ck = ck_ref[0, 0, pl.ds(j0, width), :]
        s0 = s[:, :tq] - ck[:, 0:1]
        s1 = s[:, tq:] - ck[:, 1:2]
        if masked:
            c = (j0 - pre) + lax.broadcasted_iota(jnp.int32, (width, tq), 0)
            r = i * tq + lax.broadcasted_iota(jnp.int32, (width, tq), 1)
            ok = c <= r
            s0 = jnp.where(ok, s0, MASK_VALUE)
            s1 = jnp.where(ok, s1, MASK_VALUE)
        s = jnp.concatenate([s0, s1], axis=1)
        m_old = m_ref[...]
        m_new = jnp.maximum(m_old, jnp.max(s, axis=0, keepdims=True))
        p = jnp.exp(s - m_new)
        alpha = jnp.exp(m_old - m_new)
        l_ref[...] = alpha * l_ref[...] + jnp.sum(p, axis=0, keepdims=True)
        acc_ref[...] = alpha * acc_ref[...] + jnp.dot(vtj, p.astype(BF16), preferred_element_type=F32)
        m_ref[...] = m_new

    if pre:
        block(0, pre, False)
    nfull = i * (tq // tk)

    def body(j, carry):
        block(pl.multiple_of(pre + j * tk, LANES), tk, False)
        return carry

    lax.fori_loop(0, nfull, body, 0)
    for dblk in range(tq // tk):
        block(pl.multiple_of(pre + i * tq + dblk * tk, LANES), tk, True)

    ot = acc_ref[...] / l_ref[...]
    top = lax.broadcasted_iota(jnp.int32, (LANES, tq), 0) < FOX_DH
    o = jnp.where(top, ot[:, :tq], ot[:, tq:]).T
    o_ref[0] = (o * jax.nn.sigmoid(g_ref[0])).astype(o_ref.dtype)


def fox_prompt_attention(q, kb, vt, gate, ck, pre):
    b, tq_all, d = q.shape
    tk_all = kb.shape[1]
    hp = d // LANES
    tq = _tile(tq_all, 512, LANES)
    tk = tq
    return pl.pallas_call(
        functools.partial(_fox_prompt_kernel, tk=tk, pre=pre),
        grid=(b, hp, tq_all // tq),
        in_specs=[
            pl.BlockSpec((1, tq, LANES), lambda bi, hi, qi: (bi, qi, hi)),
            pl.BlockSpec((1, tk_all, LANES), lambda bi, hi, qi: (bi, 0, hi)),
            pl.BlockSpec((1, 1, LANES, tk_all), lambda bi, hi, qi: (bi, hi, 0, 0)),
            pl.BlockSpec((1, tq, LANES), lambda bi, hi, qi: (bi, qi, hi)),
            pl.BlockSpec((1, 1, tk_all, 2), lambda bi, hi, qi: (bi, hi, 0, 0)),
        ],
        out_specs=pl.BlockSpec((1, tq, LANES), lambda bi, hi, qi: (bi, qi, hi)),
        out_shape=jax.ShapeDtypeStruct((b, tq_all, d), BF16),
        scratch_shapes=[
            pltpu.VMEM((1, 2 * tq), F32),
            pltpu.VMEM((1, 2 * tq), F32),
            pltpu.VMEM((LANES, 2 * tq), F32),
        ],
        compiler_params=_cparams(("parallel", "parallel", "arbitrary")),
        name="fox_prompt_attention",
    )(q, kb, vt, gate, ck)


def _fox_decode_kernel(pt_ref, q_ref, g_ref, kn_ref, vn_ref, cn_ref, tri_ref, *rest, pg, ds):
    k_refs = rest[:pg]
    v_refs = rest[pg:2 * pg]
    lf_refs = rest[2 * pg:3 * pg]
    o_ref = rest[3 * pg]
    qbd_ref, ks_ref, vs_ref, m_ref, l_ref, acc_ref, carry_ref = rest[3 * pg + 1:]
    j = pl.program_id(1)
    nh = FOX_HEADS
    nr = ds * nh
    d = q_ref.shape[2]
    page = k_refs[0].shape[1]

    def head_mask(shape, row_axis, lane_axis):
        hrow = lax.broadcasted_iota(jnp.int32, shape, row_axis) & (nh - 1)
        hlane = lax.broadcasted_iota(jnp.int32, shape, lane_axis) >> 6
        return hrow == hlane

    @pl.when(j == 0)
    def _():
        q = q_ref[0].astype(F32)
        qrep = jnp.concatenate(
            [jnp.broadcast_to(q[qi:qi + 1, :], (nh, d)) for qi in range(ds)], axis=0)
        qbd_ref[...] = jnp.where(head_mask((nr, d), 0, 1), qrep, 0.0).astype(BF16)
        m_ref[...] = jnp.full_like(m_ref, MASK_VALUE)
        l_ref[...] = jnp.zeros_like(l_ref)
        acc_ref[...] = jnp.zeros_like(acc_ref)
        carry_ref[...] = jnp.zeros_like(carry_ref)

    def update(s, v_bf):
        m_old = m_ref[...]
        m_new = jnp.maximum(m_old, jnp.max(s, axis=-1, keepdims=True))
        p = jnp.exp(s - m_new)
        alpha = jnp.exp(m_old - m_new)
        l_ref[...] = alpha * l_ref[...] + jnp.sum(p, axis=-1, keepdims=True)
        acc_ref[...] = alpha * acc_ref[...] + jnp.dot(p.astype(BF16), v_bf, preferred_element_type=F32)
        m_ref[...] = m_new

    parts = []
    for pi in range(pg):
        ks_ref[pi * page:(pi + 1) * page, :] = k_refs[pi][0].astype(BF16)
        vs_ref[pi * page:(pi + 1) * page, :] = v_refs[pi][0].astype(BF16)
        parts.extend(_split3(lf_refs[pi][0]))
    w = jnp.dot(jnp.concatenate(parts, axis=0), tri_ref[...], preferred_element_type=F32)
    carry = carry_ref[...]
    cums = []
    for pi in range(pg):
        base = 3 * pi * nh
        wp = w[base:base + nh] + w[base + nh:base + 2 * nh] + w[base + 2 * nh:base + 3 * nh]
        cp = carry + wp
        cums.append(cp)
        carry = cp[:, page - 1:page]
    carry_ref[...] = carry
    ck = jnp.concatenate(cums, axis=-1)
    bias = jnp.concatenate([ck] * ds, axis=0)

    s = lax.dot_general(qbd_ref[...], ks_ref[...], (((1,), (1,)), ((), ())),
                        preferred_element_type=F32) - bias
    update(s, vs_ref[...])

    @pl.when(j == pl.num_programs(1) - 1)
    def _():
        kn = kn_ref[0]
        vn = vn_ref[0]
        cn = carry_ref[...] + cn_ref[0]
        sn = lax.dot_general(qbd_ref[...], kn, (((1,), (1,)), ((), ())),
                             preferred_element_type=F32) - jnp.concatenate([cn] * ds, axis=0)
        qi = lax.broadcasted_iota(jnp.int32, (nr, page), 0) >> 4
        kj = lax.broadcasted_iota(jnp.int32, (nr, page), 1)
        sn = jnp.where(kj <= qi, sn, MASK_VALUE)
        update(sn, vn)
        acc = acc_ref[...] / l_ref[...]
        acc = jnp.where(head_mask((nr, d), 0, 1), acc, 0.0)
        o = jnp.concatenate(
            [jnp.sum(acc[qi_ * nh:(qi_ + 1) * nh, :], axis=0, keepdims=True) for qi_ in range(ds)], axis=0)
        o_ref[0] = (o * jax.nn.sigmoid(g_ref[0])).astype(o_ref.dtype)


def fox_decode_attention(page_table, q, gate, k_new, v_new, cum_new_t, tri_u, cache_k, cache_v, cache_lf_t):
    db, ds, d = q.shape
    npages = page_table.shape[1]
    page = cache_k.shape[1]
    pg = _tile(npages, PAGES_PER_STEP, 1)
    nh = FOX_HEADS
    nr = ds * nh

    def page_spec(shape, pi):
        return pl.BlockSpec(shape, lambda bi, ji, pt: (pt[bi, ji * pg + pi], 0, 0))

    per_b3 = lambda bi, ji, pt: (bi, 0, 0)
    in_specs = [
        pl.BlockSpec((1, ds, d), per_b3),
        pl.BlockSpec((1, ds, d), per_b3),
        pl.BlockSpec((1, page, d), per_b3),
        pl.BlockSpec((1, page, d), per_b3),
        pl.BlockSpec((1, nh, page), per_b3),
        pl.BlockSpec((page, page), lambda bi, ji, pt: (0, 0)),
    ]
    in_specs += [page_spec((1, page, d), pi) for pi in range(pg)]
    in_specs += [page_spec((1, page, d), pi) for pi in range(pg)]
    in_specs += [page_spec((1, nh, page), pi) for pi in range(pg)]
    grid_spec = pltpu.PrefetchScalarGridSpec(
        num_scalar_prefetch=1,
        grid=(db, npages // pg),
        in_specs=in_specs,
        out_specs=pl.BlockSpec((1, ds, d), per_b3),
        scratch_shapes=[
            pltpu.VMEM((nr, d), BF16),
            pltpu.VMEM((pg * page, d), BF16),
            pltpu.VMEM((pg * page, d), BF16),
            pltpu.VMEM((nr, 1), F32),
            pltpu.VMEM((nr, 1), F32),
            pltpu.VMEM((nr, d), F32),
            pltpu.VMEM((nh, 1), F32),
        ],
    )
    return pl.pallas_call(
        functools.partial(_fox_decode_kernel, pg=pg, ds=ds),
        grid_spec=grid_spec,
        out_shape=jax.ShapeDtypeStruct((db, ds, d), F32),
        compiler_params=_cparams(("parallel", "arbitrary")),
        name="fox_decode_attention",
    )(page_table, q, gate, k_new, v_new, cum_new_t, tri_u,
      *([cache_k] * pg), *([cache_v] * pg), *([cache_lf_t] * pg))


def _block_diag_ones(n, blk):
    i = jnp.arange(n)
    return (i[:, None] // blk == i[None, :] // blk).astype(BF16)


def _lower_tri(n, seq):
    i = jnp.arange(n)
    return ((i[:, None] >= i[None, :]) & (i[:, None] // seq == i[None, :] // seq)).astype(BF16)


def _prep_weights(w_ret_in, w_ret_out, w_kvf, b_f, g_k, w_fox_qg, g_q, w_fox_out, w_mlp_up, w_mlp_down):
    d = w_kvf.shape[0]
    nh = FOX_HEADS
    wf = jnp.zeros((d, LANES), BF16).at[:, :nh].set(w_kvf[:, 2 * d:].astype(BF16))
    bf = jnp.zeros((1, LANES), F32).at[0, :nh].set(b_f)
    return dict(
        ret_in=w_ret_in.astype(BF16), ret_out=w_ret_out.astype(BF16),
        wk=w_kvf[:, :d].astype(BF16), wv=w_kvf[:, d:2 * d].astype(BF16), wf=wf, bf=bf,
        gk_t=jnp.tile(g_k, nh).reshape(1, d),
        wq=w_fox_qg[:, :, :d].astype(BF16), wg=w_fox_qg[:, :, d:].astype(BF16),
        gq_t=(jnp.tile(g_q, (1, nh)) * (FOX_DH ** -0.5)).reshape(-1, 1, d),
        fox_out=w_fox_out.astype(BF16), up=w_mlp_up.astype(BF16), down=w_mlp_down.astype(BF16),
        bd=_block_diag_ones(MXU_DIM, FOX_DH),
    )


def _prompt_forward(x_prompt, meta, g_attn, g_mlp, w):
    b, seq, d = x_prompt.shape
    pad = RET_CHUNK - N_META
    x = jnp.concatenate([jnp.zeros((b, pad, d), F32),
                         jnp.broadcast_to(meta[None], (b, N_META, d)), x_prompt], axis=1)
    t = x.shape[1]
    pos = jnp.arange(t) - pad
    valid = pos >= 0
    x = x.reshape(b * t, d)
    n_ret = w["ret_in"].shape[0]
    states = []
    for l in range(n_ret):
        p = norm_matmul(x, g_attn[l], w["ret_in"][l])
        s0 = jnp.zeros((b, RET_HEADS, 256, 512), F32)
        og, s_new = retention(p, s0, RET_CHUNK, pos, valid, BF16)
        states.append(s_new)
        x = proj_mlp(x, og, w["ret_out"][l], g_mlp[l], w["up"][l], w["down"][l])

    tm = _tile(t, 512, LANES)
    k, v, logf, _, ckm, kb, vb = kv_proj(x, w["g_kv"], w["wk"], w["wv"], w["wf"], w["bf"], w["gk_t"],
                                         w["bd"], _lower_tri(tm, tm), b, pad)
    pre = RET_CHUNK
    nh = FOX_HEADS
    ck = ckm.reshape(b, t, nh // 2, 2).transpose(0, 2, 1, 3)
    kb3 = kb.reshape(b, t, d)
    vt = vb.reshape(b, t, d // LANES, LANES).transpose(0, 2, 3, 1)
    xr = x.reshape(b, t, d)[:, pre:].reshape(b * seq, d)
    for l in range(n_ret, g_attn.shape[0]):
        q, gate = fox_in(xr, g_attn[l], w["wq"][l - n_ret], w["wg"][l - n_ret], w["gq_t"][l - n_ret], w["bd"])
        a = fox_prompt_attention(q.reshape(b, seq, d), kb3, vt, gate.reshape(b, seq, d), ck, pre)
        xr = proj_mlp(xr, a.reshape(b * seq, d), w["fox_out"][l - n_ret], g_mlp[l], w["up"][l], w["down"][l])
    y = xr.reshape(b, seq, d)
    k4 = k.reshape(b, t, nh, FOX_DH)[:, pad:]
    v4 = v.reshape(b, t, nh, FOX_DH)[:, pad:]
    return y, jnp.stack(states), k4, v4, logf.reshape(b, t, nh)[:, pad:]


def _sample_forward(x_sample, state_ret, cache_k, cache_v, cache_logf, page_table, g_attn, g_mlp, w):
    db, ds, d = x_sample.shape
    n_pool, page, nh, dh = cache_k.shape
    past = page_table.shape[1] * page
    pos = past + jnp.arange(ds)
    valid = jnp.ones((ds,), bool)
    x = x_sample.reshape(db * ds, d)
    n_ret = w["ret_in"].shape[0]
    states = []
    for l in range(n_ret):
        p = norm_matmul(x, g_attn[l], w["ret_in"][l])
        og, s_new = retention(p, state_ret[l], ds, pos, valid, F32)
        states.append(s_new)
        x = proj_mlp(x, og, w["ret_out"][l], g_mlp[l], w["up"][l], w["down"][l])

    n = db * ds
    k, v, logf, cum, _, kb, vb = kv_proj(x, w["g_kv"], w["wk"], w["wv"], w["wf"], w["bf"], w["gk_t"],
                                         w["bd"], _lower_tri(n, ds), 1, 0)
    zrows = jnp.zeros((db, page - ds, d), BF16)
    k_new = jnp.concatenate([kb.reshape(db, ds, d), zrows], axis=1)
    v_new = jnp.concatenate([vb.reshape(db, ds, d), zrows], axis=1)
    cum_t = jnp.zeros((db, nh, page), F32).at[:, :, :ds].set(cum.reshape(db, ds, nh).transpose(0, 2, 1))
    tri_u = _lower_tri(page, page).T
    ck3 = cache_k.reshape(n_pool, page, d)
    cv3 = cache_v.reshape(n_pool, page, d)
    clf_t = cache_logf.transpose(0, 2, 1)
    for l in range(n_ret, g_attn.shape[0]):
        q, gate = fox_in(x, g_attn[l], w["wq"][l - n_ret], w["wg"][l - n_ret], w["gq_t"][l - n_ret], w["bd"])
        a = fox_decode_attention(page_table, q.reshape(db, ds, d), gate.reshape(db, ds, d),
                                 k_new, v_new, cum_t, tri_u, ck3, cv3, clf_t)
        x = proj_mlp(x, a.reshape(n, d), w["fox_out"][l - n_ret], g_mlp[l], w["up"][l], w["down"][l])
    return (x.reshape(db, ds, d), jnp.stack(states), k.reshape(db, ds, nh, dh), v.reshape(db, ds, nh, dh),
            logf.reshape(db, ds, nh))


def kernel(x_prompt, x_sample, state_ret, cache_k, cache_v, cache_logf, page_table, meta, g_attn, g_mlp,
           w_ret_in, w_ret_out, g_kv, w_kvf, b_f, g_k, w_fox_qg, g_q, w_fox_out, w_mlp_up, w_mlp_down):
    w = _prep_weights(w_ret_in, w_ret_out, w_kvf, b_f, g_k, w_fox_qg, g_q, w_fox_out, w_mlp_up, w_mlp_down)
    w["g_kv"] = g_kv
    y_p, s_p, k_p, v_p, lf_p = _prompt_forward(x_prompt, meta, g_attn, g_mlp, w)
    y_s, s_s, k_s, v_s, lf_s = _sample_forward(x_sample, state_ret, cache_k, cache_v, cache_logf, page_table,
                                               g_attn, g_mlp, w)
    return (y_p, y_s, s_p, s_s, k_p, v_p, lf_p, k_s, v_s, lf_s)
```

```python
import functools

import jax
import jax.numpy as jnp
from jax import lax
from jax.experimental import pallas as pl
from jax.experimental.pallas import tpu as pltpu

F32 = jnp.float32
BF16 = jnp.bfloat16

N_META = 16
RET_HEADS = 4
RET_CHUNK = 128
ROPE_BASE = 10000.0
FOX_HEADS = 16
FOX_DH = 64
EPS = 1e-6
MASK_VALUE = -1e30

LANES = 128
MXU_DIM = 256
VMEM_LIMIT = 56 * 1024 * 1024
PAGES_PER_STEP = 8


def _cparams(sem):
    return pltpu.CompilerParams(dimension_semantics=sem, vmem_limit_bytes=VMEM_LIMIT)


def _tile(n, target, mult=8):
    best = None
    for t in range(mult, min(n, target) + 1, mult):
        if n % t == 0:
            best = t
    assert best is not None, (n, target, mult)
    return best


def _rms(x, g_row):
    ms = jnp.mean(x * x, axis=-1, keepdims=True)
    return x * lax.rsqrt(ms + EPS) * g_row


def _split2(x):
    hi = x.astype(BF16)
    lo = (x - hi.astype(F32)).astype(BF16)
    return hi, lo


def _split3(x):
    hi = x.astype(BF16)
    r = x - hi.astype(F32)
    mid = r.astype(BF16)
    lo = (r - mid.astype(F32)).astype(BF16)
    return hi, mid, lo


def _head_rms(x, bd_ref, g_row):
    xx = x * x
    hi, lo = _split2(xx)
    bd = bd_ref[...]
    parts = []
    for c in range(x.shape[1] // MXU_DIM):
        sl = slice(c * MXU_DIM, (c + 1) * MXU_DIM)
        parts.append(jnp.dot(hi[:, sl], bd, preferred_element_type=F32)
                     + jnp.dot(lo[:, sl], bd, preferred_element_type=F32))
    ss = jnp.concatenate(parts, axis=-1)
    return x * lax.rsqrt(ss * (1.0 / FOX_DH) + EPS) * g_row


def _norm_matmul_kernel(x_ref, g_ref, w_ref, o_ref, xn_ref):
    @pl.when(pl.program_id(1) == 0)
    def _():
        xn_ref[...] = _rms(x_ref[...], g_ref[...]).astype(BF16)

    o_ref[...] = jnp.dot(xn_ref[...], w_ref[...], preferred_element_type=F32)


def norm_matmul(x, g, w):
    n, d = x.shape
    nout = w.shape[1]
    tm = _tile(n, 1536)
    tn = _tile(nout, 1024, LANES)
    return pl.pallas_call(
        _norm_matmul_kernel,
        grid=(n // tm, nout // tn),
        in_specs=[
            pl.BlockSpec((tm, d), lambda i, j: (i, 0)),
            pl.BlockSpec((1, d), lambda i, j: (0, 0)),
            pl.BlockSpec((d, tn), lambda i, j: (0, j)),
        ],
        out_specs=pl.BlockSpec((tm, tn), lambda i, j: (i, j)),
        out_shape=jax.ShapeDtypeStruct((n, nout), F32),
        scratch_shapes=[pltpu.VMEM((tm, d), BF16)],
        compiler_params=_cparams(("parallel", "arbitrary")),
        name="norm_matmul",
    )(x, g.reshape(1, d), w)


def _retention_kernel(q_ref, k_ref, v_ref, gt_ref, cq_ref, sq_ref, ck_ref, sk_ref,
                      dm_ref, qd_ref, kd_ref, gc_ref, s0_ref, og_ref, s_ref):
    @pl.when(pl.program_id(1) == 0)
    def _():
        s_ref[0] = s0_ref[0, 0]

    nh, dk, dv = s_ref.shape[1:]
    half = dk // 2
    cq, sq = cq_ref[...], sq_ref[...]
    ck, sk = ck_ref[...], sk_ref[...]
    for h in range(nh):
        q1 = q_ref[:, h * dk:h * dk + half]
        q2 = q_ref[:, h * dk + half:(h + 1) * dk]
        k1 = k_ref[:, h * dk:h * dk + half]
        k2 = k_ref[:, h * dk + half:(h + 1) * dk]
        qr = jnp.concatenate([q1 * cq - q2 * sq, q2 * cq + q1 * sq], axis=-1)
        kr = jnp.concatenate([k1 * ck - k2 * sk, k2 * ck + k1 * sk], axis=-1)
        qb = qr.astype(BF16)
        kb = kr.astype(BF16)
        vb = v_ref[:, h * dv:(h + 1) * dv].astype(BF16)
        s_old = s_ref[0, h]

        scores = lax.dot_general(qb, kb, (((1,), (1,)), ((), ())),
                                 preferred_element_type=F32) * dm_ref[h]
        intra = jnp.dot(scores.astype(BF16), vb, preferred_element_type=F32)
        cross = jnp.dot(qb, s_old.astype(BF16), preferred_element_type=F32) * qd_ref[h]
        o = intra + cross

        kd = (kr * kd_ref[h]).astype(BF16)
        s_ref[0, h] = gc_ref[h] * s_old + lax.dot_general(
            kd, vb, (((0,), (0,)), ((), ())), preferred_element_type=F32)

        on = o * lax.rsqrt(jnp.mean(o * o, axis=-1, keepdims=True) + EPS)
        gt = gt_ref[:, h * dv:(h + 1) * dv]
        og_ref[:, h * dv:(h + 1) * dv] = (gt * jax.nn.sigmoid(gt) * on).astype(og_ref.dtype)


def _retention_tables(chunk, pos, valid):
    dk = 256
    half = dk // 2
    lg = jnp.log1p(-jnp.exp2(-5.0 - jnp.arange(RET_HEADS, dtype=F32)))
    idx = jnp.arange(chunk, dtype=F32)
    diff = idx[:, None] - idx[None, :]
    dmat = jnp.where(diff >= 0, jnp.exp(lg[:, None, None] * jnp.maximum(diff, 0.0)), 0.0)
    qdec = jnp.exp(lg[:, None] * (idx[None, :] + 1.0))[:, :, None]
    kdec = jnp.exp(lg[:, None] * (chunk - 1.0 - idx[None, :]))[:, :, None]
    gc = jnp.exp(lg * chunk)[:, None, None]
    inv_freq = ROPE_BASE ** (-jnp.arange(half, dtype=F32) / half)
    ang = pos.astype(F32)[:, None] * inv_freq[None, :]
    cos, sin = jnp.cos(ang), jnp.sin(ang)
    kscale = (dk ** -0.5) * valid.astype(F32)[:, None]
    return dmat, qdec, kdec, gc, cos, sin, cos * kscale, sin * kscale


def retention(p, s0_all, layer, chunk, pos, valid, out_dtype):
    _, b, h, dk, dv = s0_all.shape
    n = p.shape[0]
    t = n // b
    nc = t // chunk
    dmat, qdec, kdec, gc, cq, sq, ck, sk = _retention_tables(chunk, pos, valid)
    row = lambda bi, ci: bi * nc + ci
    kblk = (h * dk) // (h * dk)
    vblk = (2 * h * dk) // (h * dv)
    gblk = (2 * h * dk + h * dv) // (h * dv)
    rope_spec = pl.BlockSpec((chunk, dk // 2), lambda bi, ci: (ci, 0))
    full3 = lambda bi, ci: (0, 0, 0)
    return pl.pallas_call(
        _retention_kernel,
        grid=(b, nc),
        in_specs=[
            pl.BlockSpec((chunk, h * dk), lambda bi, ci: (row(bi, ci), 0)),
            pl.BlockSpec((chunk, h * dk), lambda bi, ci: (row(bi, ci), kblk)),
            pl.BlockSpec((chunk, h * dv), lambda bi, ci: (row(bi, ci), vblk)),
            pl.BlockSpec((chunk, h * dv), lambda bi, ci: (row(bi, ci), gblk)),
            rope_spec, rope_spec, rope_spec, rope_spec,
            pl.BlockSpec((h, chunk, chunk), full3),
            pl.BlockSpec((h, chunk, 1), full3),
            pl.BlockSpec((h, chunk, 1), full3),
            pl.BlockSpec((h, 1, 1), full3),
            pl.BlockSpec((1, 1, h, dk, dv), lambda bi, ci: (layer, bi, 0, 0, 0)),
        ],
        out_specs=[
            pl.BlockSpec((chunk, h * dv), lambda bi, ci: (row(bi, ci), 0)),
            pl.BlockSpec((1, h, dk, dv), lambda bi, ci: (bi, 0, 0, 0)),
        ],
        out_shape=[
            jax.ShapeDtypeStruct((n, h * dv), out_dtype),
            jax.ShapeDtypeStruct((b, h, dk, dv), F32),
        ],
        compiler_params=_cparams(("parallel", "arbitrary")),
        name="retention",
    )(p, p, p, p, cq, sq, ck, sk, dmat, qdec, kdec, gc, s0_all)


def _proj_mlp_kernel(x_ref, a_ref, wo_ref, g_ref, wu_ref, wd_ref, o_ref, xn_ref):
    @pl.when(pl.program_id(1) == 0)
    def _():
        x1 = x_ref[...] + jnp.dot(a_ref[...].astype(BF16), wo_ref[...], preferred_element_type=F32)
        o_ref[...] = x1
        xn_ref[...] = _rms(x1, g_ref[...]).astype(BF16)

    hdn = jnp.dot(xn_ref[...], wu_ref[...], preferred_element_type=F32)
    hdn = jnp.square(jnp.maximum(hdn, 0.0)).astype(BF16)
    o_ref[...] += jnp.dot(hdn, wd_ref[...], preferred_element_type=F32)


def proj_mlp(x, a, wo, g, wu, wd):
    n, d = x.shape
    ka = a.shape[1]
    dff = wu.shape[1]
    tm = _tile(n, 768, 16)
    tf = _tile(dff, 1024, LANES)
    return pl.pallas_call(
        _proj_mlp_kernel,
        grid=(n // tm, dff // tf),
        in_specs=[
            pl.BlockSpec((tm, d), lambda i, f: (i, 0)),
            pl.BlockSpec((tm, ka), lambda i, f: (i, 0)),
            pl.BlockSpec((ka, d), lambda i, f: (0, 0)),
            pl.BlockSpec((1, d), lambda i, f: (0, 0)),
            pl.BlockSpec((d, tf), lambda i, f: (0, f)),
            pl.BlockSpec((tf, d), lambda i, f: (f, 0)),
        ],
        out_specs=pl.BlockSpec((tm, d), lambda i, f: (i, 0)),
        out_shape=jax.ShapeDtypeStruct((n, d), F32),
        scratch_shapes=[pltpu.VMEM((tm, d), BF16)],
        compiler_params=_cparams(("parallel", "arbitrary")),
        name="proj_mlp",
    )(x, a, wo, g.reshape(1, d), wu, wd)


def _kv_kernel(x_ref, g_ref, wk_ref, wv_ref, wf_ref, bf_ref, gk_ref, bd_ref, tri_ref,
               kt_ref, vt_ref, lf_ref, cum_ref, ckm_ref, kb_ref, vtb_ref, carry_ref, *, pad):
    t = pl.program_id(1)
    tm = x_ref.shape[0]

    @pl.when(t == 0)
    def _():
        carry_ref[...] = jnp.zeros_like(carry_ref)

    xn = _rms(x_ref[...], g_ref[...]).astype(BF16)
    kraw = jnp.dot(xn, wk_ref[...], preferred_element_type=F32)
    k = _head_rms(kraw, bd_ref, gk_ref[...])
    vt = lax.dot_general(wv_ref[...], xn, (((1,), (1,)), ((), ())), preferred_element_type=F32)
    kt_ref[0] = k.T
    vt_ref[0] = vt
    kb_ref[...] = k.astype(BF16)
    vtb_ref[0] = vt.astype(BF16)

    z = jnp.dot(xn, wf_ref[...], preferred_element_type=F32) + bf_ref[...]
    logf = jnp.minimum(z, 0.0) - jnp.log1p(jnp.exp(-jnp.abs(z)))
    rows = t * tm + lax.broadcasted_iota(jnp.int32, (tm, LANES), 0)
    valid = rows >= pad
    logf = jnp.where(valid, logf, 0.0)
    hi, mid, lo = _split3(logf)
    tri = tri_ref[...]
    cum = carry_ref[...] + (jnp.dot(tri, hi, preferred_element_type=F32)
                            + jnp.dot(tri, mid, preferred_element_type=F32)
                            + jnp.dot(tri, lo, preferred_element_type=F32))
    carry_ref[...] = cum[tm - 1:tm, :]
    nh = lf_ref.shape[1]
    lf_ref[...] = logf[:, :nh]
    cum_ref[...] = cum[:, :nh]
    ckm_ref[...] = jnp.where(valid, cum, -MASK_VALUE)[:, :nh]


def kv_proj(x, g, wk, wv, wf, bf, gk_t, bd, tri, nb, pad):
    n, d = x.shape
    tm = tri.shape[0]
    tb = n // nb
    nt = max(tb // tm, 1)
    nbg = n // (tm * nt)
    nh = FOX_HEADS
    row = lambda bi, ti: (bi * nt + ti, 0)
    col = lambda bi, ti: (bi, 0, ti)
    full = lambda bi, ti: (0, 0)
    return pl.pallas_call(
        functools.partial(_kv_kernel, pad=pad),
        grid=(nbg, nt),
        in_specs=[
            pl.BlockSpec((tm, d), row),
            pl.BlockSpec((1, d), full),
            pl.BlockSpec((d, d), full),
            pl.BlockSpec((d, d), full),
            pl.BlockSpec((d, LANES), full),
            pl.BlockSpec((1, LANES), full),
            pl.BlockSpec((1, d), full),
            pl.BlockSpec((MXU_DIM, MXU_DIM), full),
            pl.BlockSpec((tm, tm), full),
        ],
        out_specs=[
            pl.BlockSpec((1, d, tm), col),
            pl.BlockSpec((1, d, tm), col),
            pl.BlockSpec((tm, nh), row),
            pl.BlockSpec((tm, nh), row),
            pl.BlockSpec((tm, nh), row),
            pl.BlockSpec((tm, d), row),
            pl.BlockSpec((1, d, tm), col),
        ],
        out_shape=[
            jax.ShapeDtypeStruct((nbg, d, nt * tm), F32),
            jax.ShapeDtypeStruct((nbg, d, nt * tm), F32),
            jax.ShapeDtypeStruct((n, nh), F32),
            jax.ShapeDtypeStruct((n, nh), F32),
            jax.ShapeDtypeStruct((n, nh), F32),
            jax.ShapeDtypeStruct((n, d), BF16),
            jax.ShapeDtypeStruct((nbg, d, nt * tm), BF16),
        ],
        scratch_shapes=[pltpu.VMEM((1, LANES), F32)],
        compiler_params=_cparams(("parallel", "arbitrary")),
        name="kv_proj",
    )(x, g.reshape(1, d), wk, wv, wf, bf, gk_t, bd, tri)


def _fox_in_kernel(x_ref, g_ref, wq_ref, wg_ref, gq_ref, bd_ref, q_ref, gate_ref):
    xn = _rms(x_ref[...], g_ref[...]).astype(BF16)
    qraw = jnp.dot(xn, wq_ref[...], preferred_element_type=F32)
    q_ref[...] = _head_rms(qraw, bd_ref, gq_ref[...]).astype(BF16)
    gate_ref[...] = jnp.dot(xn, wg_ref[...], preferred_element_type=F32)


def fox_in(x, g, wq, wg, gq_t, bd):
    n, d = x.shape
    tm = _tile(n, 1024, 16)
    row = lambda i: (i, 0)
    full = lambda i: (0, 0)
    return pl.pallas_call(
        _fox_in_kernel,
        grid=(n // tm,),
        in_specs=[
            pl.BlockSpec((tm, d), row),
            pl.BlockSpec((1, d), full),
            pl.BlockSpec((d, d), full),
            pl.BlockSpec((d, d), full),
            pl.BlockSpec((1, d), full),
            pl.BlockSpec((MXU_DIM, MXU_DIM), full),
        ],
        out_specs=[pl.BlockSpec((tm, d), row), pl.BlockSpec((tm, d), row)],
        out_shape=[jax.ShapeDtypeStruct((n, d), BF16), jax.ShapeDtypeStruct((n, d), F32)],
        compiler_params=_cparams(("parallel",)),
        name="fox_in",
    )(x, g.reshape(1, d), wq, wg, gq_t, bd)


def _fox_prompt_kernel(q_ref, k_ref, vt_ref, g_ref, ck_ref, o_ref, m_ref, l_ref, acc_ref, *, tk, pre):
    i = pl.program_id(2)
    tq = q_ref.shape[1]
    q = q_ref[0]
    low = lax.broadcasted_iota(jnp.int32, (tq, LANES), 1) < FOX_DH
    zero = jnp.zeros_like(q)
    qs = jnp.concatenate([jnp.where(low, q, zero), jnp.where(low, zero, q)], axis=0)

    m_ref[...] = jnp.full_like(m_ref, MASK_VALUE)
    l_ref[...] = jnp.zeros_like(l_ref)
    acc_ref[...] = jnp.zeros_like(acc_ref)

    def block(j0, width, masked):
        kj = k_ref[0, pl.ds(j0, width), :]
        vtj = vt_ref[0, 0, :, pl.ds(j0, width)]
        s = lax.dot_general(kj, qs, (((1,), (1,)), ((), ())), preferred_element_type=F32)
        ck = ck_ref[0, 0, pl.ds(j0, width), :]
        s0 = s[:, :tq] - ck[:, 0:1]
        s1 = s[:, tq:] - ck[:, 1:2]
        if masked:
            c = (j0 - pre) + lax.broadcasted_iota(jnp.int32, (width, tq), 0)
            r = i * tq + lax.broadcasted_iota(jnp.int32, (width, tq), 1)
            ok = c <= r
            s0 = jnp.where(ok, s0, MASK_VALUE)
            s1 = jnp.where(ok, s1, MASK_VALUE)
        s = jnp.concatenate([s0, s1], axis=1)
        m_old = m_ref[...]
        m_new = jnp.maximum(m_old, jnp.max(s, axis=0, keepdims=True))
        p = jnp.exp(s - m_new)
        alpha = jnp.exp(m_old - m_new)
        l_ref[...] = alpha * l_ref[...] + jnp.sum(p, axis=0, keepdims=True)
        acc_ref[...] = alpha * acc_ref[...] + jnp.dot(vtj, p.astype(BF16), preferred_element_type=F32)
        m_ref[...] = m_new

    if pre:
        block(0, pre, False)
    nfull = i * (tq // tk)

    def body(j, carry):
        block(pl.multiple_of(pre + j * tk, LANES), tk, False)
        return carry

    lax.fori_loop(0, nfull, body, 0)
    for dblk in range(tq // tk):
        block(pl.multiple_of(pre + i * tq + dblk * tk, LANES), tk, True)

    ot = acc_ref[...] / l_ref[...]
    top = lax.broadcasted_iota(jnp.int32, (LANES, tq), 0) < FOX_DH
    o = jnp.where(top, ot[:, :tq], ot[:, tq:]).T
    o_ref[0] = (o * jax.nn.sigmoid(g_ref[0])).astype(o_ref.dtype)


def fox_prompt_attention(q, kb, vt, gate, ck, pre):
    b, tq_all, d = q.shape
    tk_all = kb.shape[1]
    hp = d // LANES
    tq = _tile(tq_all, 512, LANES)
    tk = tq
    return pl.pallas_call(
        functools.partial(_fox_prompt_kernel, tk=tk, pre=pre),
        grid=(b, hp, tq_all // tq),
        in_specs=[
            pl.BlockSpec((1, tq, LANES), lambda bi, hi, qi: (bi, qi, hi)),
            pl.BlockSpec((1, tk_all, LANES), lambda bi, hi, qi: (bi, 0, hi)),
            pl.BlockSpec((1, 1, LANES, tk_all), lambda bi, hi, qi: (bi, hi, 0, 0)),
            pl.BlockSpec((1, tq, LANES), lambda bi, hi, qi: (bi, qi, hi)),
            pl.BlockSpec((1, 1, tk_all, 2), lambda bi, hi, qi: (bi, hi, 0, 0)),
        ],
        out_specs=pl.BlockSpec((1, tq, LANES), lambda bi, hi, qi: (bi, qi, hi)),
        out_shape=jax.ShapeDtypeStruct((b, tq_all, d), BF16),
        scratch_shapes=[
            pltpu.VMEM((1, 2 * tq), F32),
            pltpu.VMEM((1, 2 * tq), F32),
            pltpu.VMEM((LANES, 2 * tq), F32),
        ],
        compiler_params=_cparams(("parallel", "parallel", "arbitrary")),
        name="fox_prompt_attention",
    )(q, kb, vt, gate, ck)


def _fox_decode_kernel(pt_ref, q_ref, g_ref, kn_ref, vn_ref, cn_ref, tri_ref, *rest, pg, ds):
    k_refs = rest[:pg]
    v_refs = rest[pg:2 * pg]
    lf_refs = rest[2 * pg:3 * pg]
    o_ref = rest[3 * pg]
    qbd_ref, ks_ref, vs_ref, m_ref, l_ref, acc_ref, carry_ref = rest[3 * pg + 1:]
    j = pl.program_id(1)
    nh = FOX_HEADS
    nr = ds * nh
    d = q_ref.shape[2]
    page = k_refs[0].shape[2]

    def head_mask(shape, row_axis, lane_axis):
        hrow = lax.broadcasted_iota(jnp.int32, shape, row_axis) & (nh - 1)
        hlane = lax.broadcasted_iota(jnp.int32, shape, lane_axis) >> 6
        return hrow == hlane

    @pl.when(j == 0)
    def _():
        q = q_ref[0].astype(F32)
        qrep = jnp.concatenate(
            [jnp.broadcast_to(q[qi:qi + 1, :], (nh, d)) for qi in range(ds)], axis=0)
        qbd_ref[...] = jnp.where(head_mask((nr, d), 0, 1), qrep, 0.0).astype(BF16)
        m_ref[...] = jnp.full_like(m_ref, MASK_VALUE)
        l_ref[...] = jnp.zeros_like(l_ref)
        acc_ref[...] = jnp.zeros_like(acc_ref)
        carry_ref[...] = jnp.zeros_like(carry_ref)

    def update(s, v_bf, v_key_axis):
        m_old = m_ref[...]
        m_new = jnp.maximum(m_old, jnp.max(s, axis=-1, keepdims=True))
        p = jnp.exp(s - m_new)
        alpha = jnp.exp(m_old - m_new)
        l_ref[...] = alpha * l_ref[...] + jnp.sum(p, axis=-1, keepdims=True)
        acc_ref[...] = alpha * acc_ref[...] + lax.dot_general(
            p.astype(BF16), v_bf, (((1,), (v_key_axis,)), ((), ())), preferred_element_type=F32)
        m_ref[...] = m_new

    parts = []
    for pi in range(pg):
        ks_ref[:, pi * page:(pi + 1) * page] = k_refs[pi][0].astype(BF16)
        vs_ref[:, pi * page:(pi + 1) * page] = v_refs[pi][0].astype(BF16)
        parts.extend(_split3(lf_refs[pi][0]))
    w = jnp.dot(jnp.concatenate(parts, axis=0), tri_ref[...], preferred_element_type=F32)
    carry = carry_ref[...]
    cums = []
    for pi in range(pg):
        base = 3 * pi * nh
        wp = w[base:base + nh] + w[base + nh:base + 2 * nh] + w[base + 2 * nh:base + 3 * nh]
        cp = carry + wp
        cums.append(cp)
        carry = cp[:, page - 1:page]
    carry_ref[...] = carry
    ck = jnp.concatenate(cums, axis=-1)
    bias = jnp.concatenate([ck] * ds, axis=0)

    s = jnp.dot(qbd_ref[...], ks_ref[...], preferred_element_type=F32) - bias
    update(s, vs_ref[...], 1)

    @pl.when(j == pl.num_programs(1) - 1)
    def _():
        kn = kn_ref[0]
        vn = vn_ref[0]
        cn = carry_ref[...] + cn_ref[0]
        sn = lax.dot_general(qbd_ref[...], kn, (((1,), (1,)), ((), ())),
                             preferred_element_type=F32) - jnp.concatenate([cn] * ds, axis=0)
        qi = lax.broadcasted_iota(jnp.int32, (nr, page), 0) >> 4
        kj = lax.broadcasted_iota(jnp.int32, (nr, page), 1)
        sn = jnp.where(kj <= qi, sn, MASK_VALUE)
        update(sn, vn, 0)
        acc = acc_ref[...] / l_ref[...]
        acc = jnp.where(head_mask((nr, d), 0, 1), acc, 0.0)
        o = jnp.concatenate(
            [jnp.sum(acc[qi_ * nh:(qi_ + 1) * nh, :], axis=0, keepdims=True) for qi_ in range(ds)], axis=0)
        o_ref[0] = (o * jax.nn.sigmoid(g_ref[0])).astype(o_ref.dtype)


def fox_decode_attention(page_table, q, gate, k_new, v_new, cum_new_t, tri_u, cache_k, cache_v, cache_lf_t):
    db, ds, d = q.shape
    npages = page_table.shape[1]
    page = cache_k.shape[2]
    pg = _tile(npages, PAGES_PER_STEP, 1)
    nh = FOX_HEADS
    nr = ds * nh

    def page_spec(shape, pi):
        return pl.BlockSpec(shape, lambda bi, ji, pt: (pt[bi, ji * pg + pi], 0, 0))

    per_b3 = lambda bi, ji, pt: (bi, 0, 0)
    in_specs = [
        pl.BlockSpec((1, ds, d), per_b3),
        pl.BlockSpec((1, ds, d), per_b3),
        pl.BlockSpec((1, page, d), per_b3),
        pl.BlockSpec((1, page, d), per_b3),
        pl.BlockSpec((1, nh, page), per_b3),
        pl.BlockSpec((page, page), lambda bi, ji, pt: (0, 0)),
    ]
    in_specs += [page_spec((1, d, page), pi) for pi in range(pg)]
    in_specs += [page_spec((1, d, page), pi) for pi in range(pg)]
    in_specs += [page_spec((1, nh, page), pi) for pi in range(pg)]
    grid_spec = pltpu.PrefetchScalarGridSpec(
        num_scalar_prefetch=1,
        grid=(db, npages // pg),
        in_specs=in_specs,
        out_specs=pl.BlockSpec((1, ds, d), per_b3),
        scratch_shapes=[
            pltpu.VMEM((nr, d), BF16),
            pltpu.VMEM((d, pg * page), BF16),
            pltpu.VMEM((d, pg * page), BF16),
            pltpu.VMEM((nr, 1), F32),
            pltpu.VMEM((nr, 1), F32),
            pltpu.VMEM((nr, d), F32),
            pltpu.VMEM((nh, 1), F32),
        ],
    )
    return pl.pallas_call(
        functools.partial(_fox_decode_kernel, pg=pg, ds=ds),
        grid_spec=grid_spec,
        out_shape=jax.ShapeDtypeStruct((db, ds, d), F32),
        compiler_params=_cparams(("parallel", "arbitrary")),
        name="fox_decode_attention",
    )(page_table, q, gate, k_new, v_new, cum_new_t, tri_u,
      *([cache_k] * pg), *([cache_v] * pg), *([cache_lf_t] * pg))


def _block_diag_ones(n, blk):
    i = jnp.arange(n)
    return (i[:, None] // blk == i[None, :] // blk).astype(BF16)


def _lower_tri(n, seq):
    i = jnp.arange(n)
    return ((i[:, None] >= i[None, :]) & (i[:, None] // seq == i[None, :] // seq)).astype(BF16)


def _prep_weights(w_ret_in, w_ret_out, w_kvf, b_f, g_k, w_fox_qg, g_q, w_fox_out, w_mlp_up, w_mlp_down):
    d = w_kvf.shape[0]
    nh = FOX_HEADS
    wf = jnp.zeros((d, LANES), BF16).at[:, :nh].set(w_kvf[:, 2 * d:].astype(BF16))
    bf = jnp.zeros((1, LANES), F32).at[0, :nh].set(b_f)
    return dict(
        ret_in=w_ret_in.astype(BF16), ret_out=w_ret_out.astype(BF16),
        wk=w_kvf[:, :d].astype(BF16), wv=w_kvf[:, d:2 * d].T.astype(BF16), wf=wf, bf=bf,
        gk_t=jnp.tile(g_k, nh).reshape(1, d),
        wq=w_fox_qg[:, :, :d].astype(BF16), wg=w_fox_qg[:, :, d:].astype(BF16),
        gq_t=(jnp.tile(g_q, (1, nh)) * (FOX_DH ** -0.5)).reshape(-1, 1, d),
        fox_out=w_fox_out.astype(BF16), up=w_mlp_up.astype(BF16), down=w_mlp_down.astype(BF16),
        bd=_block_diag_ones(MXU_DIM, FOX_DH),
    )


def _prompt_forward(x_prompt, meta, g_attn, g_mlp, w):
    b, seq, d = x_prompt.shape
    pad = RET_CHUNK - N_META
    x = jnp.concatenate([jnp.zeros((b, pad, d), F32),
                         jnp.broadcast_to(meta[None], (b, N_META, d)), x_prompt], axis=1)
    t = x.shape[1]
    pos = jnp.arange(t) - pad
    valid = pos >= 0
    x = x.reshape(b * t, d)
    n_ret = w["ret_in"].shape[0]
    states = []
    for l in range(n_ret):
        p = norm_matmul(x, g_attn[l], w["ret_in"][l])
        s0 = jnp.zeros((1, b, RET_HEADS, 256, 512), F32)
        og, s_new = retention(p, s0, 0, RET_CHUNK, pos, valid, BF16)
        states.append(s_new)
        x = proj_mlp(x, og, w["ret_out"][l], g_mlp[l], w["up"][l], w["down"][l])

    tm = _tile(t, 512, LANES)
    kt, vt32, logf, _, ckm, kb, vtb = kv_proj(x, w["g_kv"], w["wk"], w["wv"], w["wf"], w["bf"], w["gk_t"],
                                              w["bd"], _lower_tri(tm, tm), b, pad)
    pre = RET_CHUNK
    nh = FOX_HEADS
    ck = ckm.reshape(b, t, nh // 2, 2).transpose(0, 2, 1, 3)
    kb3 = kb.reshape(b, t, d)
    vt = vtb.reshape(b, d // LANES, LANES, t)
    xr = x.reshape(b, t, d)[:, pre:].reshape(b * seq, d)
    for l in range(n_ret, g_attn.shape[0]):
        q, gate = fox_in(xr, g_attn[l], w["wq"][l - n_ret], w["wg"][l - n_ret], w["gq_t"][l - n_ret], w["bd"])
        a = fox_prompt_attention(q.reshape(b, seq, d), kb3, vt, gate.reshape(b, seq, d), ck, pre)
        xr = proj_mlp(xr, a.reshape(b * seq, d), w["fox_out"][l - n_ret], g_mlp[l], w["up"][l], w["down"][l])
    y = xr.reshape(b, seq, d)
    k4 = kt.reshape(b, nh, FOX_DH, t)[:, :, :, pad:].transpose(0, 3, 1, 2)
    v4 = vt32.reshape(b, nh, FOX_DH, t)[:, :, :, pad:].transpose(0, 3, 1, 2)
    return y, jnp.stack(states), k4, v4, logf.reshape(b, t, nh)[:, pad:]


def _sample_forward(x_sample, state_ret, cache_k, cache_v, cache_logf, page_table, g_attn, g_mlp, w):
    db, ds, d = x_sample.shape
    n_pool, page, nh, dh = cache_k.shape
    past = page_table.shape[1] * page
    pos = past + jnp.arange(ds)
    valid = jnp.ones((ds,), bool)
    x = x_sample.reshape(db * ds, d)
    n_ret = w["ret_in"].shape[0]
    states = []
    for l in range(n_ret):
        p = norm_matmul(x, g_attn[l], w["ret_in"][l])
        og, s_new = retention(p, state_ret, l, ds, pos, valid, F32)
        states.append(s_new)
        x = proj_mlp(x, og, w["ret_out"][l], g_mlp[l], w["up"][l], w["down"][l])

    n = db * ds
    kt, vt32, logf, cum, _, kb, vtb = kv_proj(x, w["g_kv"], w["wk"], w["wv"], w["wf"], w["bf"], w["gk_t"],
                                              w["bd"], _lower_tri(n, ds), 1, 0)
    k, v = kt[0].T, vt32[0].T
    zrows = jnp.zeros((db, page - ds, d), BF16)
    k_new = jnp.concatenate([kb.reshape(db, ds, d), zrows], axis=1)
    v_new = jnp.concatenate([vtb[0].T.reshape(db, ds, d), zrows], axis=1)
    cum_t = jnp.zeros((db, nh, page), F32).at[:, :, :ds].set(cum.reshape(db, ds, nh).transpose(0, 2, 1))
    tri_u = _lower_tri(page, page).T
    ck3 = cache_k.transpose(0, 2, 3, 1).reshape(n_pool, d, page)
    cv3 = cache_v.transpose(0, 2, 3, 1).reshape(n_pool, d, page)
    clf_t = cache_logf.transpose(0, 2, 1)
    for l in range(n_ret, g_attn.shape[0]):
        q, gate = fox_in(x, g_attn[l], w["wq"][l - n_ret], w["wg"][l - n_ret], w["gq_t"][l - n_ret], w["bd"])
        a = fox_decode_attention(page_table, q.reshape(db, ds, d), gate.reshape(db, ds, d),
                                 k_new, v_new, cum_t, tri_u, ck3, cv3, clf_t)
        x = proj_mlp(x, a.reshape(n, d), w["fox_out"][l - n_ret], g_mlp[l], w["up"][l], w["down"][l])
    return (x.reshape(db, ds, d), jnp.stack(states), k.reshape(db, ds, nh, dh), v.reshape(db, ds, nh, dh),
            logf.reshape(db, ds, nh))


def kernel(x_prompt, x_sample, state_ret, cache_k, cache_v, cache_logf, page_table, meta, g_attn, g_mlp,
           w_ret_in, w_ret_out, g_kv, w_kvf, b_f, g_k, w_fox_qg, g_q, w_fox_out, w_mlp_up, w_mlp_down):
    w = _prep_weights(w_ret_in, w_ret_out, w_kvf, b_f, g_k, w_fox_qg, g_q, w_fox_out, w_mlp_up, w_mlp_down)
    w["g_kv"] = g_kv
    y_p, s_p, k_p, v_p, lf_p = _prompt_forward(x_prompt, meta, g_attn, g_mlp, w)
    y_s, s_s, k_s, v_s, lf_s = _sample_forward(x_sample, state_ret, cache_k, cache_v, cache_logf, page_table,
                                               g_attn, g_mlp, w)
    return (y_p, y_s, s_p, s_s, k_p, v_p, lf_p, k_s, v_s, lf_s)
```

```python
import functools

import jax
import jax.numpy as jnp
from jax import lax
from jax.experimental import pallas as pl
from jax.experimental.pallas import tpu as pltpu

F32 = jnp.float32
BF16 = jnp.bfloat16

N_META = 16
RET_HEADS = 4
RET_CHUNK = 128
ROPE_BASE = 10000.0
FOX_HEADS = 16
FOX_DH = 64
EPS = 1e-6
MASK_VALUE = -1e30

LANES = 128
MXU_DIM = 256
VMEM_LIMIT = 56 * 1024 * 1024
PAGES_PER_STEP = 8
DENOM_ROWS = 16
LOG2E = 1.4426950408889634


def _cparams(sem):
    return pltpu.CompilerParams(dimension_semantics=sem, vmem_limit_bytes=VMEM_LIMIT)


def _tile(n, target, mult=8):
    best = None
    for t in range(mult, min(n, target) + 1, mult):
        if n % t == 0:
            best = t
    assert best is not None, (n, target, mult)
    return best


def _rms(x, g_row):
    ms = jnp.mean(x * x, axis=-1, keepdims=True)
    return x * lax.rsqrt(ms + EPS) * g_row


def _split2(x):
    hi = x.astype(BF16)
    lo = (x - hi.astype(F32)).astype(BF16)
    return hi, lo


def _split3(x):
    hi = x.astype(BF16)
    r = x - hi.astype(F32)
    mid = r.astype(BF16)
    lo = (r - mid.astype(F32)).astype(BF16)
    return hi, mid, lo


def _head_rms(x, bd_ref, g_row):
    xx = x * x
    hi, lo = _split2(xx)
    bd = bd_ref[...]
    parts = []
    for c in range(x.shape[1] // MXU_DIM):
        sl = slice(c * MXU_DIM, (c + 1) * MXU_DIM)
        parts.append(jnp.dot(hi[:, sl], bd, preferred_element_type=F32)
                     + jnp.dot(lo[:, sl], bd, preferred_element_type=F32))
    ss = jnp.concatenate(parts, axis=-1)
    return x * lax.rsqrt(ss * (1.0 / FOX_DH) + EPS) * g_row


def _norm_matmul_kernel(x_ref, g_ref, w_ref, o_ref, xn_ref):
    @pl.when(pl.program_id(1) == 0)
    def _():
        xn_ref[...] = _rms(x_ref[...], g_ref[...]).astype(BF16)

    o_ref[...] = jnp.dot(xn_ref[...], w_ref[...], preferred_element_type=F32)


def norm_matmul(x, g, w):
    n, d = x.shape
    nout = w.shape[1]
    tm = _tile(n, 1536)
    tn = _tile(nout, 1024, LANES)
    return pl.pallas_call(
        _norm_matmul_kernel,
        grid=(n // tm, nout // tn),
        in_specs=[
            pl.BlockSpec((tm, d), lambda i, j: (i, 0)),
            pl.BlockSpec((1, d), lambda i, j: (0, 0)),
            pl.BlockSpec((d, tn), lambda i, j: (0, j)),
        ],
        out_specs=pl.BlockSpec((tm, tn), lambda i, j: (i, j)),
        out_shape=jax.ShapeDtypeStruct((n, nout), F32),
        scratch_shapes=[pltpu.VMEM((tm, d), BF16)],
        compiler_params=_cparams(("parallel", "arbitrary")),
        name="norm_matmul",
    )(x, g.reshape(1, d), w)


def _retention_kernel(q_ref, k_ref, v_ref, gt_ref, cq_ref, sq_ref, ck_ref, sk_ref,
                      dm_ref, qd_ref, kd_ref, gc_ref, s0_ref, og_ref, s_ref):
    @pl.when(pl.program_id(1) == 0)
    def _():
        s_ref[0] = s0_ref[0, 0]

    nh, dk, dv = s_ref.shape[1:]
    half = dk // 2
    cq, sq = cq_ref[...], sq_ref[...]
    ck, sk = ck_ref[...], sk_ref[...]
    for h in range(nh):
        q1 = q_ref[:, h * dk:h * dk + half]
        q2 = q_ref[:, h * dk + half:(h + 1) * dk]
        k1 = k_ref[:, h * dk:h * dk + half]
        k2 = k_ref[:, h * dk + half:(h + 1) * dk]
        qr = jnp.concatenate([q1 * cq - q2 * sq, q2 * cq + q1 * sq], axis=-1)
        kr = jnp.concatenate([k1 * ck - k2 * sk, k2 * ck + k1 * sk], axis=-1)
        qb = qr.astype(BF16)
        kb = kr.astype(BF16)
        vb = v_ref[:, h * dv:(h + 1) * dv].astype(BF16)
        s_old = s_ref[0, h]

        scores = lax.dot_general(qb, kb, (((1,), (1,)), ((), ())),
                                 preferred_element_type=F32) * dm_ref[h]
        intra = jnp.dot(scores.astype(BF16), vb, preferred_element_type=F32)
        cross = jnp.dot(qb, s_old.astype(BF16), preferred_element_type=F32) * qd_ref[h]
        o = intra + cross

        kd = (kr * kd_ref[h]).astype(BF16)
        s_ref[0, h] = gc_ref[h] * s_old + lax.dot_general(
            kd, vb, (((0,), (0,)), ((), ())), preferred_element_type=F32)

        on = o * lax.rsqrt(jnp.mean(o * o, axis=-1, keepdims=True) + EPS)
        gt = gt_ref[:, h * dv:(h + 1) * dv]
        og_ref[:, h * dv:(h + 1) * dv] = (gt * jax.nn.sigmoid(gt) * on).astype(og_ref.dtype)


def _retention_tables(chunk, pos, valid):
    dk = 256
    half = dk // 2
    lg = jnp.log1p(-jnp.exp2(-5.0 - jnp.arange(RET_HEADS, dtype=F32)))
    idx = jnp.arange(chunk, dtype=F32)
    diff = idx[:, None] - idx[None, :]
    dmat = jnp.where(diff >= 0, jnp.exp(lg[:, None, None] * jnp.maximum(diff, 0.0)), 0.0)
    qdec = jnp.exp(lg[:, None] * (idx[None, :] + 1.0))[:, :, None]
    kdec = jnp.exp(lg[:, None] * (chunk - 1.0 - idx[None, :]))[:, :, None]
    gc = jnp.exp(lg * chunk)[:, None, None]
    inv_freq = ROPE_BASE ** (-jnp.arange(half, dtype=F32) / half)
    ang = pos.astype(F32)[:, None] * inv_freq[None, :]
    cos, sin = jnp.cos(ang), jnp.sin(ang)
    kscale = (dk ** -0.5) * valid.astype(F32)[:, None]
    return dmat, qdec, kdec, gc, cos, sin, cos * kscale, sin * kscale


def retention(p, s0_all, layer, chunk, pos, valid, out_dtype):
    _, b, h, dk, dv = s0_all.shape
    n = p.shape[0]
    t = n // b
    nc = t // chunk
    dmat, qdec, kdec, gc, cq, sq, ck, sk = _retention_tables(chunk, pos, valid)
    row = lambda bi, ci: bi * nc + ci
    kblk = (h * dk) // (h * dk)
    vblk = (2 * h * dk) // (h * dv)
    gblk = (2 * h * dk + h * dv) // (h * dv)
    rope_spec = pl.BlockSpec((chunk, dk // 2), lambda bi, ci: (ci, 0))
    full3 = lambda bi, ci: (0, 0, 0)
    return pl.pallas_call(
        _retention_kernel,
        grid=(b, nc),
        in_specs=[
            pl.BlockSpec((chunk, h * dk), lambda bi, ci: (row(bi, ci), 0)),
            pl.BlockSpec((chunk, h * dk), lambda bi, ci: (row(bi, ci), kblk)),
            pl.BlockSpec((chunk, h * dv), lambda bi, ci: (row(bi, ci), vblk)),
            pl.BlockSpec((chunk, h * dv), lambda bi, ci: (row(bi, ci), gblk)),
            rope_spec, rope_spec, rope_spec, rope_spec,
            pl.BlockSpec((h, chunk, chunk), full3),
            pl.BlockSpec((h, chunk, 1), full3),
            pl.BlockSpec((h, chunk, 1), full3),
            pl.BlockSpec((h, 1, 1), full3),
            pl.BlockSpec((1, 1, h, dk, dv), lambda bi, ci: (layer, bi, 0, 0, 0)),
        ],
        out_specs=[
            pl.BlockSpec((chunk, h * dv), lambda bi, ci: (row(bi, ci), 0)),
            pl.BlockSpec((1, h, dk, dv), lambda bi, ci: (bi, 0, 0, 0)),
        ],
        out_shape=[
            jax.ShapeDtypeStruct((n, h * dv), out_dtype),
            jax.ShapeDtypeStruct((b, h, dk, dv), F32),
        ],
        compiler_params=_cparams(("parallel", "arbitrary")),
        name="retention",
    )(p, p, p, p, cq, sq, ck, sk, dmat, qdec, kdec, gc, s0_all)


def _proj_mlp_kernel(x_ref, a_ref, wo_ref, g_ref, wu_ref, wd_ref, o_ref, xn_ref):
    @pl.when(pl.program_id(1) == 0)
    def _():
        x1 = x_ref[...] + jnp.dot(a_ref[...].astype(BF16), wo_ref[...], preferred_element_type=F32)
        o_ref[...] = x1
        xn_ref[...] = _rms(x1, g_ref[...]).astype(BF16)

    hdn = jnp.dot(xn_ref[...], wu_ref[...], preferred_element_type=F32)
    hdn = jnp.square(jnp.maximum(hdn, 0.0)).astype(BF16)
    o_ref[...] += jnp.dot(hdn, wd_ref[...], preferred_element_type=F32)


def proj_mlp(x, a, wo, g, wu, wd):
    n, d = x.shape
    ka = a.shape[1]
    dff = wu.shape[1]
    tm = _tile(n, 768, 16)
    tf = _tile(dff, 1024, LANES)
    return pl.pallas_call(
        _proj_mlp_kernel,
        grid=(n // tm, dff // tf),
        in_specs=[
            pl.BlockSpec((tm, d), lambda i, f: (i, 0)),
            pl.BlockSpec((tm, ka), lambda i, f: (i, 0)),
            pl.BlockSpec((ka, d), lambda i, f: (0, 0)),
            pl.BlockSpec((1, d), lambda i, f: (0, 0)),
            pl.BlockSpec((d, tf), lambda i, f: (0, f)),
            pl.BlockSpec((tf, d), lambda i, f: (f, 0)),
        ],
        out_specs=pl.BlockSpec((tm, d), lambda i, f: (i, 0)),
        out_shape=jax.ShapeDtypeStruct((n, d), F32),
        scratch_shapes=[pltpu.VMEM((tm, d), BF16)],
        compiler_params=_cparams(("parallel", "arbitrary")),
        name="proj_mlp",
    )(x, a, wo, g.reshape(1, d), wu, wd)


def _kv_kernel(x_ref, g_ref, wk_ref, wv_ref, wf_ref, bf_ref, gk_ref, bd_ref, tri_ref,
               kt_ref, vt_ref, lf_ref, cum_ref, ckm_ref, kb_ref, vtb_ref, carry_ref, *, pad):
    t = pl.program_id(1)
    tm = x_ref.shape[0]

    @pl.when(t == 0)
    def _():
        carry_ref[...] = jnp.zeros_like(carry_ref)

    xn = _rms(x_ref[...], g_ref[...]).astype(BF16)
    kraw = jnp.dot(xn, wk_ref[...], preferred_element_type=F32)
    k = _head_rms(kraw, bd_ref, gk_ref[...])
    vt = lax.dot_general(wv_ref[...], xn, (((1,), (1,)), ((), ())), preferred_element_type=F32)
    kt_ref[0] = k.T
    vt_ref[0] = vt
    kb_ref[...] = k.astype(BF16)
    vtb_ref[0] = vt.astype(BF16)

    z = jnp.dot(xn, wf_ref[...], preferred_element_type=F32) + bf_ref[...]
    logf = jnp.minimum(z, 0.0) - jnp.log1p(jnp.exp(-jnp.abs(z)))
    rows = t * tm + lax.broadcasted_iota(jnp.int32, (tm, LANES), 0)
    valid = rows >= pad
    logf = jnp.where(valid, logf, 0.0)
    hi, mid, lo = _split3(logf)
    tri = tri_ref[...]
    cum = carry_ref[...] + (jnp.dot(tri, hi, preferred_element_type=F32)
                            + jnp.dot(tri, mid, preferred_element_type=F32)
                            + jnp.dot(tri, lo, preferred_element_type=F32))
    carry_ref[...] = cum[tm - 1:tm, :]
    nh = lf_ref.shape[1]
    lf_ref[...] = logf[:, :nh]
    cum_ref[...] = cum[:, :nh]
    ckm_ref[...] = jnp.where(valid, cum * LOG2E, -MASK_VALUE)[:, :nh]


def kv_proj(x, g, wk, wv, wf, bf, gk_t, bd, tri, nb, pad):
    n, d = x.shape
    tm = tri.shape[0]
    tb = n // nb
    nt = max(tb // tm, 1)
    nbg = n // (tm * nt)
    nh = FOX_HEADS
    row = lambda bi, ti: (bi * nt + ti, 0)
    col = lambda bi, ti: (bi, 0, ti)
    full = lambda bi, ti: (0, 0)
    return pl.pallas_call(
        functools.partial(_kv_kernel, pad=pad),
        grid=(nbg, nt),
        in_specs=[
            pl.BlockSpec((tm, d), row),
            pl.BlockSpec((1, d), full),
            pl.BlockSpec((d, d), full),
            pl.BlockSpec((d, d), full),
            pl.BlockSpec((d, LANES), full),
            pl.BlockSpec((1, LANES), full),
            pl.BlockSpec((1, d), full),
            pl.BlockSpec((MXU_DIM, MXU_DIM), full),
            pl.BlockSpec((tm, tm), full),
        ],
        out_specs=[
            pl.BlockSpec((1, d, tm), col),
            pl.BlockSpec((1, d, tm), col),
            pl.BlockSpec((tm, nh), row),
            pl.BlockSpec((tm, nh), row),
            pl.BlockSpec((tm, nh), row),
            pl.BlockSpec((tm, d), row),
            pl.BlockSpec((1, d, tm), col),
        ],
        out_shape=[
            jax.ShapeDtypeStruct((nbg, d, nt * tm), F32),
            jax.ShapeDtypeStruct((nbg, d, nt * tm), F32),
            jax.ShapeDtypeStruct((n, nh), F32),
            jax.ShapeDtypeStruct((n, nh), F32),
            jax.ShapeDtypeStruct((n, nh), F32),
            jax.ShapeDtypeStruct((n, d), BF16),
            jax.ShapeDtypeStruct((nbg, d, nt * tm), BF16),
        ],
        scratch_shapes=[pltpu.VMEM((1, LANES), F32)],
        compiler_params=_cparams(("parallel", "arbitrary")),
        name="kv_proj",
    )(x, g.reshape(1, d), wk, wv, wf, bf, gk_t, bd, tri)


def _fox_in_kernel(x_ref, g_ref, wq_ref, wg_ref, gq_ref, bd_ref, q_ref, gate_ref):
    xn = _rms(x_ref[...], g_ref[...]).astype(BF16)
    qraw = jnp.dot(xn, wq_ref[...], preferred_element_type=F32)
    q_ref[...] = _head_rms(qraw, bd_ref, gq_ref[...]).astype(BF16)
    gate_ref[...] = jnp.dot(xn, wg_ref[...], preferred_element_type=F32)


def fox_in(x, g, wq, wg, gq_t, bd):
    n, d = x.shape
    tm = _tile(n, 1024, 16)
    row = lambda i: (i, 0)
    full = lambda i: (0, 0)
    return pl.pallas_call(
        _fox_in_kernel,
        grid=(n // tm,),
        in_specs=[
            pl.BlockSpec((tm, d), row),
            pl.BlockSpec((1, d), full),
            pl.BlockSpec((d, d), full),
            pl.BlockSpec((d, d), full),
            pl.BlockSpec((1, d), full),
            pl.BlockSpec((MXU_DIM, MXU_DIM), full),
        ],
        out_specs=[pl.BlockSpec((tm, d), row), pl.BlockSpec((tm, d), row)],
        out_shape=[jax.ShapeDtypeStruct((n, d), BF16), jax.ShapeDtypeStruct((n, d), F32)],
        compiler_params=_cparams(("parallel",)),
        name="fox_in",
    )(x, g.reshape(1, d), wq, wg, gq_t, bd)


def _fox_prompt_kernel(q_ref, k_ref, vt_ref, g_ref, ck_ref, o_ref, m_ref, acc_ref, sa_ref, sb_ref, *, pre):
    i = pl.program_id(2)
    tq = q_ref.shape[1]
    tk = tq
    q = q_ref[0]
    low = lax.broadcasted_iota(jnp.int32, (tq, LANES), 1) < FOX_DH
    zero = jnp.zeros_like(q)
    qs = jnp.concatenate([jnp.where(low, q, zero), jnp.where(low, zero, q)], axis=0)

    m_ref[...] = jnp.full_like(m_ref, MASK_VALUE)
    acc_ref[...] = jnp.zeros_like(acc_ref)

    def scores(j0, width):
        kj = k_ref[0, pl.ds(j0, width), :]
        s = lax.dot_general(kj, qs, (((1,), (1,)), ((), ())), preferred_element_type=F32)
        ck = ck_ref[0, 0, pl.ds(j0, width), :]
        return s[:, :tq] - ck[:, 0:1], s[:, tq:] - ck[:, 1:2]

    def start(b):
        return pl.multiple_of(pre + b * tk, LANES)

    def qk(b, s_ref):
        s0, s1 = scores(start(b), tk)
        s_ref[:, :tq] = s0
        s_ref[:, tq:] = s1

    def soft(s, vtj):
        m_old = m_ref[...]
        m_new = jnp.maximum(m_old, jnp.max(s, axis=0, keepdims=True))
        p = jnp.exp2(s - m_new).astype(BF16)
        alpha = jnp.exp2(m_old - m_new)
        vt_ones = jnp.concatenate([vtj, jnp.ones((DENOM_ROWS, vtj.shape[1]), BF16)], axis=0)
        acc_ref[...] = alpha * acc_ref[...] + jnp.dot(vt_ones, p, preferred_element_type=F32)
        m_ref[...] = m_new

    def soft_block(b, s_ref):
        soft(s_ref[...], vt_ref[0, 0, :, pl.ds(start(b), tk)])

    odd = (i & 1) == 1

    @pl.when(odd)
    def _():
        qk(0, sb_ref)
        qk(1, sa_ref)
        soft_block(0, sb_ref)

    @pl.when(jnp.logical_not(odd))
    def _():
        qk(0, sa_ref)

    base = i & 1

    def body(jj, carry):
        b0 = base + 2 * jj
        qk(b0 + 1, sb_ref)
        soft_block(b0, sa_ref)
        qk(b0 + 2, sa_ref)
        soft_block(b0 + 1, sb_ref)
        return carry

    lax.fori_loop(0, i >> 1, body, 0)

    key = lax.broadcasted_iota(jnp.int32, (tk, tq), 0)
    qry = lax.broadcasted_iota(jnp.int32, (tk, tq), 1)
    ok = key <= qry
    sd0 = jnp.where(ok, sa_ref[:, :tq], MASK_VALUE)
    sd1 = jnp.where(ok, sa_ref[:, tq:], MASK_VALUE)
    sp0, sp1 = scores(0, pre)
    s_last = jnp.concatenate([jnp.concatenate([sp0, sd0], axis=0), jnp.concatenate([sp1, sd1], axis=0)], axis=1)
    vt_last = jnp.concatenate([vt_ref[0, 0, :, 0:pre], vt_ref[0, 0, :, pl.ds(start(i), tk)]], axis=1)
    soft(s_last, vt_last)

    ot = acc_ref[0:LANES, :] / acc_ref[LANES:LANES + 1, :]
    top =lax.broadcasted_iota(jnp.int32, (LANES, tq), 0) < FOX_DH
    o = jnp.where(top, ot[:, :tq], ot[:, tq:]).T
    o_ref[0] = (o * jax.nn.sigmoid(g_ref[0])).astype(o_ref.dtype)


def fox_prompt_attention(q, kb, vt, gate, ck, pre):
    b, tq_all, d = q.shape
    tk_all = kb.shape[1]
    hp = d // LANES
    tq = _tile(tq_all, 512, LANES)
    return pl.pallas_call(
        functools.partial(_fox_prompt_kernel, pre=pre),
        grid=(b, hp, tq_all // tq),
        in_specs=[
            pl.BlockSpec((1, tq, LANES), lambda bi, hi, qi: (bi, qi, hi)),
            pl.BlockSpec((1, tk_all, LANES), lambda bi, hi, qi: (bi, 0, hi)),
            pl.BlockSpec((1, 1, LANES, tk_all), lambda bi, hi, qi: (bi, hi, 0, 0)),
            pl.BlockSpec((1, tq, LANES), lambda bi, hi, qi: (bi, qi, hi)),
            pl.BlockSpec((1, 1, tk_all, 2), lambda bi, hi, qi: (bi, hi, 0, 0)),
        ],
        out_specs=pl.BlockSpec((1, tq, LANES), lambda bi, hi, qi: (bi, qi, hi)),
        out_shape=jax.ShapeDtypeStruct((b, tq_all, d), BF16),
        scratch_shapes=[
            pltpu.VMEM((1, 2 * tq), F32),
            pltpu.VMEM((LANES + DENOM_ROWS, 2 * tq), F32),
            pltpu.VMEM((tq, 2 * tq), F32),
            pltpu.VMEM((tq, 2 * tq), F32),
        ],
        compiler_params=_cparams(("parallel", "parallel", "arbitrary")),
        name="fox_prompt_attention",
    )(q, kb, vt, gate, ck)


def _fox_decode_kernel(pt_ref, q_ref, g_ref, kn_ref, vn_ref, cn_ref, tri_ref, *rest, pg, ds):
    k_refs = rest[:pg]
    v_refs = rest[pg:2 * pg]
    lf_refs = rest[2 * pg:3 * pg]
    o_ref = rest[3 * pg]
    qbd_ref, ks_ref, vs_ref, m_ref, l_ref, acc_ref, carry_ref = rest[3 * pg + 1:]
    j = pl.program_id(1)
    nh = FOX_HEADS
    nr = ds * nh
    d = q_ref.shape[2]
    page = k_refs[0].shape[2]

    def head_mask(shape, row_axis, lane_axis):
        hrow = lax.broadcasted_iota(jnp.int32, shape, row_axis) & (nh - 1)
        hlane = lax.broadcasted_iota(jnp.int32, shape, lane_axis) >> 6
        return hrow == hlane

    @pl.when(j == 0)
    def _():
        q = q_ref[0].astype(F32)
        qrep = jnp.concatenate(
            [jnp.broadcast_to(q[qi:qi + 1, :], (nh, d)) for qi in range(ds)], axis=0)
        qbd_ref[...] = jnp.where(head_mask((nr, d), 0, 1), qrep, 0.0).astype(BF16)
        m_ref[...] = jnp.full_like(m_ref, MASK_VALUE)
        l_ref[...] = jnp.zeros_like(l_ref)
        acc_ref[...] = jnp.zeros_like(acc_ref)
        carry_ref[...] = jnp.zeros_like(carry_ref)

    def update(s, v_bf, v_key_axis):
        m_old = m_ref[...]
        m_new = jnp.maximum(m_old, jnp.max(s, axis=-1, keepdims=True))
        p = jnp.exp2(s - m_new)
        alpha = jnp.exp2(m_old - m_new)
        l_ref[...] = alpha * l_ref[...] + jnp.sum(p, axis=-1, keepdims=True)
        acc_ref[...] = alpha * acc_ref[...] + lax.dot_general(
            p.astype(BF16), v_bf, (((1,), (v_key_axis,)), ((), ())), preferred_element_type=F32)
        m_ref[...] = m_new

    parts = []
    for pi in range(pg):
        ks_ref[:, pi * page:(pi + 1) * page] = k_refs[pi][0].astype(BF16)
        vs_ref[:, pi * page:(pi + 1) * page] = v_refs[pi][0].astype(BF16)
        parts.extend(_split3(lf_refs[pi][0]))
    w = jnp.dot(jnp.concatenate(parts, axis=0), tri_ref[...], preferred_element_type=F32)
    carry = carry_ref[...]
    cums = []
    for pi in range(pg):
        base = 3 * pi * nh
        wp = w[base:base + nh] + w[base + nh:base + 2 * nh] + w[base + 2 * nh:base + 3 * nh]
        cp = carry + wp
        cums.append(cp)
        carry = cp[:, page - 1:page]
    carry_ref[...] = carry
    ck = jnp.concatenate(cums, axis=-1) * LOG2E
    bias = jnp.concatenate([ck] * ds, axis=0)

    s = jnp.dot(qbd_ref[...], ks_ref[...], preferred_element_type=F32) - bias
    update(s, vs_ref[...], 1)

    @pl.when(j == pl.num_programs(1) - 1)
    def _():
        kn = kn_ref[0]
        vn = vn_ref[0]
        cn = (carry_ref[...] + cn_ref[0]) * LOG2E
        sn = lax.dot_general(qbd_ref[...], kn, (((1,), (1,)), ((), ())),
                             preferred_element_type=F32) - jnp.concatenate([cn] * ds, axis=0)
        qi = lax.broadcasted_iota(jnp.int32, (nr, page), 0) >> 4
        kj = lax.broadcasted_iota(jnp.int32, (nr, page), 1)
        sn = jnp.where(kj <= qi, sn, MASK_VALUE)
        update(sn, vn, 0)
        acc = acc_ref[...] / l_ref[...]
        acc = jnp.where(head_mask((nr, d), 0, 1), acc, 0.0)
        o = jnp.concatenate(
            [jnp.sum(acc[qi_ * nh:(qi_ + 1) * nh, :], axis=0, keepdims=True) for qi_ in range(ds)], axis=0)
        o_ref[0] = (o * jax.nn.sigmoid(g_ref[0])).astype(o_ref.dtype)


def fox_decode_attention(page_table, q, gate, k_new, v_new, cum_new_t, tri_u, cache_k, cache_v, cache_lf_t):
    db, ds, d = q.shape
    npages = page_table.shape[1]
    page = cache_k.shape[2]
    pg = _tile(npages, PAGES_PER_STEP, 1)
    nh = FOX_HEADS
    nr = ds * nh

    def page_spec(shape, pi):
        return pl.BlockSpec(shape, lambda bi, ji, pt: (pt[bi, ji * pg + pi], 0, 0))

    per_b3 = lambda bi, ji, pt: (bi, 0, 0)
    in_specs = [
        pl.BlockSpec((1, ds, d), per_b3),
        pl.BlockSpec((1, ds, d), per_b3),
        pl.BlockSpec((1, page, d), per_b3),
        pl.BlockSpec((1, page, d), per_b3),
        pl.BlockSpec((1, nh, page), per_b3),
        pl.BlockSpec((page, page), lambda bi, ji, pt: (0, 0)),
    ]
    in_specs += [page_spec((1, d, page), pi) for pi in range(pg)]
    in_specs += [page_spec((1, d, page), pi) for pi in range(pg)]
    in_specs += [page_spec((1, nh, page), pi) for pi in range(pg)]
    grid_spec = pltpu.PrefetchScalarGridSpec(
        num_scalar_prefetch=1,
        grid=(db, npages // pg),
        in_specs=in_specs,
        out_specs=pl.BlockSpec((1, ds, d), per_b3),
        scratch_shapes=[
            pltpu.VMEM((nr, d), BF16),
            pltpu.VMEM((d, pg * page), BF16),
            pltpu.VMEM((d, pg * page), BF16),
            pltpu.VMEM((nr, 1), F32),
            pltpu.VMEM((nr, 1), F32),
            pltpu.VMEM((nr, d), F32),
            pltpu.VMEM((nh, 1), F32),
        ],
    )
    return pl.pallas_call(
        functools.partial(_fox_decode_kernel, pg=pg, ds=ds),
        grid_spec=grid_spec,
        out_shape=jax.ShapeDtypeStruct((db, ds, d), F32),
        compiler_params=_cparams(("parallel", "arbitrary")),
        name="fox_decode_attention",
    )(page_table, q, gate, k_new, v_new, cum_new_t, tri_u,
      *([cache_k] * pg), *([cache_v] * pg), *([cache_lf_t] * pg))


def _block_diag_ones(n, blk):
    i = jnp.arange(n)
    return (i[:, None] // blk == i[None, :] // blk).astype(BF16)


def _lower_tri(n, seq):
    i = jnp.arange(n)
    return ((i[:, None] >= i[None, :]) & (i[:, None] // seq == i[None, :] // seq)).astype(BF16)


def _prep_weights(w_ret_in, w_ret_out, w_kvf, b_f, g_k, w_fox_qg, g_q, w_fox_out, w_mlp_up, w_mlp_down):
    d = w_kvf.shape[0]
    nh = FOX_HEADS
    wf = jnp.zeros((d, LANES), BF16).at[:, :nh].set(w_kvf[:, 2 * d:].astype(BF16))
    bf = jnp.zeros((1, LANES), F32).at[0, :nh].set(b_f)
    per_layer = lambda a: [a[l].astype(BF16) for l in range(a.shape[0])]
    return dict(
        ret_in=per_layer(w_ret_in), ret_out=per_layer(w_ret_out),
        wk=w_kvf[:, :d].astype(BF16), wv=w_kvf[:, d:2 * d].T.astype(BF16), wf=wf, bf=bf,
        gk_t=jnp.tile(g_k, nh).reshape(1, d),
        wq=per_layer(w_fox_qg[:, :, :d]), wg=per_layer(w_fox_qg[:, :, d:]),
        gq_t=(jnp.tile(g_q, (1, nh)) * (FOX_DH ** -0.5 * LOG2E)).reshape(-1, 1, d),
        fox_out=per_layer(w_fox_out), up=per_layer(w_mlp_up), down=per_layer(w_mlp_down),
        bd=_block_diag_ones(MXU_DIM, FOX_DH),
    )


def _prompt_forward(x_prompt, meta, g_attn, g_mlp, w):
    b, seq, d = x_prompt.shape
    pad = RET_CHUNK - N_META
    x = jnp.concatenate([jnp.zeros((b, pad, d), F32),
                         jnp.broadcast_to(meta[None], (b, N_META, d)), x_prompt], axis=1)
    t = x.shape[1]
    pos = jnp.arange(t) - pad
    valid = pos >= 0
    x = x.reshape(b * t, d)
    n_ret = len(w["ret_in"])
    states = []
    for l in range(n_ret):
        p = norm_matmul(x, g_attn[l], w["ret_in"][l])
        s0 = jnp.zeros((1, b, RET_HEADS, 256, 512), F32)
        og, s_new = retention(p, s0, 0, RET_CHUNK, pos, valid, BF16)
        states.append(s_new)
        x = proj_mlp(x, og, w["ret_out"][l], g_mlp[l], w["up"][l], w["down"][l])

    tm = _tile(t, 512, LANES)
    kt, vt32, logf, _, ckm, kb, vtb = kv_proj(x, w["g_kv"], w["wk"], w["wv"], w["wf"], w["bf"], w["gk_t"],
                                              w["bd"], _lower_tri(tm, tm), b, pad)
    pre = RET_CHUNK
    nh = FOX_HEADS
    ck = ckm.reshape(b, t, nh // 2, 2).transpose(0, 2, 1, 3)
    kb3 = kb.reshape(b, t, d)
    vt = vtb.reshape(b, d // LANES, LANES, t)
    xr = x.reshape(b, t, d)[:, pre:].reshape(b * seq, d)
    for l in range(n_ret, g_attn.shape[0]):
        q, gate = fox_in(xr, g_attn[l], w["wq"][l - n_ret], w["wg"][l - n_ret], w["gq_t"][l - n_ret], w["bd"])
        a = fox_prompt_attention(q.reshape(b, seq, d), kb3, vt, gate.reshape(b, seq, d), ck, pre)
        xr = proj_mlp(xr, a.reshape(b * seq, d), w["fox_out"][l - n_ret], g_mlp[l], w["up"][l], w["down"][l])
    y = xr.reshape(b, seq, d)
    k4 = kt.reshape(b, nh, FOX_DH, t)[:, :, :, pad:].transpose(0, 3, 1, 2)
    v4 = vt32.reshape(b, nh, FOX_DH, t)[:, :, :, pad:].transpose(0, 3, 1, 2)
    return y, jnp.stack(states), k4, v4, logf.reshape(b, t, nh)[:, pad:]


def _sample_forward(x_sample, state_ret, cache_k, cache_v, cache_logf, page_table, g_attn, g_mlp, w):
    db, ds, d = x_sample.shape
    n_pool, page, nh, dh = cache_k.shape
    past = page_table.shape[1] * page
    pos = past + jnp.arange(ds)
    valid = jnp.ones((ds,), bool)
    x = x_sample.reshape(db * ds, d)
    n_ret = len(w["ret_in"])
    states = []
    for l in range(n_ret):
        p = norm_matmul(x, g_attn[l], w["ret_in"][l])
        og, s_new = retention(p, state_ret, l, ds, pos, valid, F32)
        states.append(s_new)
        x = proj_mlp(x, og, w["ret_out"][l], g_mlp[l], w["up"][l], w["down"][l])

    n = db * ds
    kt, vt32, logf, cum, _, kb, vtb = kv_proj(x, w["g_kv"], w["wk"], w["wv"], w["wf"], w["bf"], w["gk_t"],
                                              w["bd"], _lower_tri(n, ds), 1, 0)
    k, v = kt[0].T, vt32[0].T
    zrows = jnp.zeros((db, page - ds, d), BF16)
    k_new = jnp.concatenate([kb.reshape(db, ds, d), zrows], axis=1)
    v_new = jnp.concatenate([vtb[0].T.reshape(db, ds, d), zrows], axis=1)
    cum_t = jnp.zeros((db, nh, page), F32).at[:, :, :ds].set(cum.reshape(db, ds, nh).transpose(0, 2, 1))
    tri_u = _lower_tri(page, page).T
    ck3 = cache_k.transpose(0, 2, 3, 1).reshape(n_pool, d, page)
    cv3 = cache_v.transpose(0, 2, 3, 1).reshape(n_pool, d, page)
    clf_t = cache_logf.transpose(0, 2, 1)
    for l in range(n_ret, g_attn.shape[0]):
        q, gate = fox_in(x, g_attn[l], w["wq"][l - n_ret], w["wg"][l - n_ret], w["gq_t"][l - n_ret], w["bd"])
        a = fox_decode_attention(page_table, q.reshape(db, ds, d), gate.reshape(db, ds, d),
                                 k_new, v_new, cum_t, tri_u, ck3, cv3, clf_t)
        x = proj_mlp(x, a.reshape(n, d), w["fox_out"][l - n_ret], g_mlp[l], w["up"][l], w["down"][l])
    return (x.reshape(db, ds, d), jnp.stack(states), k.reshape(db, ds, nh, dh), v.reshape(db, ds, nh, dh),
            logf.reshape(db, ds, nh))


def kernel(x_prompt, x_sample, state_ret, cache_k, cache_v, cache_logf, page_table, meta, g_attn, g_mlp,
           w_ret_in, w_ret_out, g_kv, w_kvf, b_f, g_k, w_fox_qg, g_q, w_fox_out, w_mlp_up, w_mlp_down):
    w = _prep_weights(w_ret_in, w_ret_out, w_kvf, b_f, g_k, w_fox_qg, g_q, w_fox_out, w_mlp_up, w_mlp_down)
    w["g_kv"] = g_kv
    y_p, s_p, k_p, v_p, lf_p = _prompt_forward(x_prompt, meta, g_attn, g_mlp, w)
    y_s, s_s, k_s, v_s, lf_s = _sample_forward(x_sample, state_ret, cache_k, cache_v, cache_logf, page_table,
                                               g_attn, g_mlp, w)
    return (y_p, y_s, s_p, s_s, k_p, v_p, lf_p, k_s, v_s, lf_s)
```

```python
import functools

import jax
import jax.numpy as jnp
from jax import lax
from jax.experimental import pallas as pl
from jax.experimental.pallas import tpu as pltpu

F32 = jnp.float32
BF16 = jnp.bfloat16

N_META = 16
RET_HEADS = 4
RET_CHUNK = 128
ROPE_BASE = 10000.0
FOX_HEADS = 16
FOX_DH = 64
EPS = 1e-6
MASK_VALUE = -1e30

LANES = 128
MXU_DIM = 256
VMEM_LIMIT = 56 * 1024 * 1024
PAGES_PER_STEP = 8
DENOM_ROWS = 16
ATTN_PAIRS_PER_STEP = 2
LOG2E = 1.4426950408889634


def _cparams(sem):
    return pltpu.CompilerParams(dimension_semantics=sem, vmem_limit_bytes=VMEM_LIMIT)


def _tile(n, target, mult=8):
    best = None
    for t in range(mult, min(n, target) + 1, mult):
        if n % t == 0:
            best = t
    assert best is not None, (n, target, mult)
    return best


def _rms(x, g_row):
    ms = jnp.mean(x * x, axis=-1, keepdims=True)
    return x * lax.rsqrt(ms + EPS) * g_row


def _split2(x):
    hi = x.astype(BF16)
    lo = (x - hi.astype(F32)).astype(BF16)
    return hi, lo


def _split3(x):
    hi = x.astype(BF16)
    r = x - hi.astype(F32)
    mid = r.astype(BF16)
    lo = (r - mid.astype(F32)).astype(BF16)
    return hi, mid, lo


def _head_rms(x, bd_ref, g_row):
    xx = x * x
    hi, lo = _split2(xx)
    bd = bd_ref[...]
    parts = []
    for c in range(x.shape[1] // MXU_DIM):
        sl = slice(c * MXU_DIM, (c + 1) * MXU_DIM)
        parts.append(jnp.dot(hi[:, sl], bd, preferred_element_type=F32)
                     + jnp.dot(lo[:, sl], bd, preferred_element_type=F32))
    ss = jnp.concatenate(parts, axis=-1)
    return x * lax.rsqrt(ss * (1.0 / FOX_DH) + EPS) * g_row


def _norm_matmul_kernel(x_ref, g_ref, w_ref, o_ref, xn_ref):
    @pl.when(pl.program_id(1) == 0)
    def _():
        xn_ref[...] = _rms(x_ref[...], g_ref[...]).astype(BF16)

    o_ref[...] = jnp.dot(xn_ref[...], w_ref[...], preferred_element_type=F32)


def norm_matmul(x, g, w):
    n, d = x.shape
    nout = w.shape[1]
    tm = _tile(n, 1536)
    tn = _tile(nout, 1024, LANES)
    return pl.pallas_call(
        _norm_matmul_kernel,
        grid=(n // tm, nout // tn),
        in_specs=[
            pl.BlockSpec((tm, d), lambda i, j: (i, 0)),
            pl.BlockSpec((1, d), lambda i, j: (0, 0)),
            pl.BlockSpec((d, tn), lambda i, j: (0, j)),
        ],
        out_specs=pl.BlockSpec((tm, tn), lambda i, j: (i, j)),
        out_shape=jax.ShapeDtypeStruct((n, nout), F32),
        scratch_shapes=[pltpu.VMEM((tm, d), BF16)],
        compiler_params=_cparams(("parallel", "arbitrary")),
        name="norm_matmul",
    )(x, g.reshape(1, d), w)


def _retention_kernel(q_ref, k_ref, v_ref, gt_ref, cq_ref, sq_ref, ck_ref, sk_ref,
                      dm_ref, qd_ref, kd_ref, gc_ref, s0_ref, og_ref, s_ref):
    @pl.when(pl.program_id(1) == 0)
    def _():
        s_ref[0] = s0_ref[0, 0]

    nh, dk, dv = s_ref.shape[1:]
    half = dk // 2
    cq, sq = cq_ref[...], sq_ref[...]
    ck, sk = ck_ref[...], sk_ref[...]
    for h in range(nh):
        q1 = q_ref[:, h * dk:h * dk + half]
        q2 = q_ref[:, h * dk + half:(h + 1) * dk]
        k1 = k_ref[:, h * dk:h * dk + half]
        k2 = k_ref[:, h * dk + half:(h + 1) * dk]
        qr = jnp.concatenate([q1 * cq - q2 * sq, q2 * cq + q1 * sq], axis=-1)
        kr = jnp.concatenate([k1 * ck - k2 * sk, k2 * ck + k1 * sk], axis=-1)
        qb = qr.astype(BF16)
        kb = kr.astype(BF16)
        vb = v_ref[:, h * dv:(h + 1) * dv].astype(BF16)
        s_old = s_ref[0, h]

        scores = lax.dot_general(qb, kb, (((1,), (1,)), ((), ())),
                                 preferred_element_type=F32) * dm_ref[h]
        intra = jnp.dot(scores.astype(BF16), vb, preferred_element_type=F32)
        cross = jnp.dot(qb, s_old.astype(BF16), preferred_element_type=F32) * qd_ref[h]
        o = intra + cross

        kd = (kr * kd_ref[h]).astype(BF16)
        s_ref[0, h] = gc_ref[h] * s_old + lax.dot_general(
            kd, vb, (((0,), (0,)), ((), ())), preferred_element_type=F32)

        on = o * lax.rsqrt(jnp.mean(o * o, axis=-1, keepdims=True) + EPS)
        gt = gt_ref[:, h * dv:(h + 1) * dv]
        og_ref[:, h * dv:(h + 1) * dv] = (gt * jax.nn.sigmoid(gt) * on).astype(og_ref.dtype)


def _retention_tables(chunk, pos, valid):
    dk = 256
    half = dk // 2
    lg = jnp.log1p(-jnp.exp2(-5.0 - jnp.arange(RET_HEADS, dtype=F32)))
    idx = jnp.arange(chunk, dtype=F32)
    diff = idx[:, None] - idx[None, :]
    dmat = jnp.where(diff >= 0, jnp.exp(lg[:, None, None] * jnp.maximum(diff, 0.0)), 0.0)
    qdec = jnp.exp(lg[:, None] * (idx[None, :] + 1.0))[:, :, None]
    kdec = jnp.exp(lg[:, None] * (chunk - 1.0 - idx[None, :]))[:, :, None]
    gc = jnp.exp(lg * chunk)[:, None, None]
    inv_freq = ROPE_BASE ** (-jnp.arange(half, dtype=F32) / half)
    ang = pos.astype(F32)[:, None] * inv_freq[None, :]
    cos, sin = jnp.cos(ang), jnp.sin(ang)
    kscale = (dk ** -0.5) * valid.astype(F32)[:, None]
    return dmat, qdec, kdec, gc, cos, sin, cos * kscale, sin * kscale


def retention(p, s0_all, layer, chunk, pos, valid, out_dtype):
    _, b, h, dk, dv = s0_all.shape
    n = p.shape[0]
    t = n // b
    nc = t // chunk
    dmat, qdec, kdec, gc, cq, sq, ck, sk = _retention_tables(chunk, pos, valid)
    row = lambda bi, ci: bi * nc + ci
    kblk = (h * dk) // (h * dk)
    vblk = (2 * h * dk) // (h * dv)
    gblk = (2 * h * dk + h * dv) // (h * dv)
    rope_spec = pl.BlockSpec((chunk, dk // 2), lambda bi, ci: (ci, 0))
    full3 = lambda bi, ci: (0, 0, 0)
    return pl.pallas_call(
        _retention_kernel,
        grid=(b, nc),
        in_specs=[
            pl.BlockSpec((chunk, h * dk), lambda bi, ci: (row(bi, ci), 0)),
            pl.BlockSpec((chunk, h * dk), lambda bi, ci: (row(bi, ci), kblk)),
            pl.BlockSpec((chunk, h * dv), lambda bi, ci: (row(bi, ci), vblk)),
            pl.BlockSpec((chunk, h * dv), lambda bi, ci: (row(bi, ci), gblk)),
            rope_spec, rope_spec, rope_spec, rope_spec,
            pl.BlockSpec((h, chunk, chunk), full3),
            pl.BlockSpec((h, chunk, 1), full3),
            pl.BlockSpec((h, chunk, 1), full3),
            pl.BlockSpec((h, 1, 1), full3),
            pl.BlockSpec((1, 1, h, dk, dv), lambda bi, ci: (layer, bi, 0, 0, 0)),
        ],
        out_specs=[
            pl.BlockSpec((chunk, h * dv), lambda bi, ci: (row(bi, ci), 0)),
            pl.BlockSpec((1, h, dk, dv), lambda bi, ci: (bi, 0, 0, 0)),
        ],
        out_shape=[
            jax.ShapeDtypeStruct((n, h * dv), out_dtype),
            jax.ShapeDtypeStruct((b, h, dk, dv), F32),
        ],
        compiler_params=_cparams(("parallel", "arbitrary")),
        name="retention",
    )(p, p, p, p, cq, sq, ck, sk, dmat, qdec, kdec, gc, s0_all)


def _proj_mlp_kernel(x_ref, a_ref, wo_ref, g_ref, wu_ref, wd_ref, o_ref, xn_ref):
    @pl.when(pl.program_id(1) == 0)
    def _():
        x1 = x_ref[...] + jnp.dot(a_ref[...].astype(BF16), wo_ref[...], preferred_element_type=F32)
        o_ref[...] = x1
        xn_ref[...] = _rms(x1, g_ref[...]).astype(BF16)

    hdn = jnp.dot(xn_ref[...], wu_ref[...], preferred_element_type=F32)
    hdn = jnp.square(jnp.maximum(hdn, 0.0)).astype(BF16)
    o_ref[...] += jnp.dot(hdn, wd_ref[...], preferred_element_type=F32)


def proj_mlp(x, a, wo, g, wu, wd):
    n, d = x.shape
    ka = a.shape[1]
    dff = wu.shape[1]
    tm = _tile(n, 768, 16)
    tf = _tile(dff, 1024, LANES)
    return pl.pallas_call(
        _proj_mlp_kernel,
        grid=(n // tm, dff // tf),
        in_specs=[
            pl.BlockSpec((tm, d), lambda i, f: (i, 0)),
            pl.BlockSpec((tm, ka), lambda i, f: (i, 0)),
            pl.BlockSpec((ka, d), lambda i, f: (0, 0)),
            pl.BlockSpec((1, d), lambda i, f: (0, 0)),
            pl.BlockSpec((d, tf), lambda i, f: (0, f)),
            pl.BlockSpec((tf, d), lambda i, f: (f, 0)),
        ],
        out_specs=pl.BlockSpec((tm, d), lambda i, f: (i, 0)),
        out_shape=jax.ShapeDtypeStruct((n, d), F32),
        scratch_shapes=[pltpu.VMEM((tm, d), BF16)],
        compiler_params=_cparams(("parallel", "arbitrary")),
        name="proj_mlp",
    )(x, a, wo, g.reshape(1, d), wu, wd)


def _kv_kernel(x_ref, g_ref, wk_ref, wv_ref, wf_ref, bf_ref, gk_ref, bd_ref, tri_ref,
               kt_ref, vt_ref, lf_ref, cum_ref, ckm_ref, kb_ref, vtb_ref, carry_ref, *, pad):
    t = pl.program_id(1)
    tm = x_ref.shape[0]

    @pl.when(t == 0)
    def _():
        carry_ref[...] = jnp.zeros_like(carry_ref)

    xn = _rms(x_ref[...], g_ref[...]).astype(BF16)
    kraw = jnp.dot(xn, wk_ref[...], preferred_element_type=F32)
    k = _head_rms(kraw, bd_ref, gk_ref[...])
    vt = lax.dot_general(wv_ref[...], xn, (((1,), (1,)), ((), ())), preferred_element_type=F32)
    kt_ref[0] = k.T
    vt_ref[0] = vt
    kb_ref[...] = k.astype(BF16)
    vtb_ref[0] = vt.astype(BF16)

    z = jnp.dot(xn, wf_ref[...], preferred_element_type=F32) + bf_ref[...]
    logf = jnp.minimum(z, 0.0) - jnp.log1p(jnp.exp(-jnp.abs(z)))
    rows = t * tm + lax.broadcasted_iota(jnp.int32, (tm, LANES), 0)
    valid = rows >= pad
    logf = jnp.where(valid, logf, 0.0)
    hi, mid, lo = _split3(logf)
    tri = tri_ref[...]
    cum = carry_ref[...] + (jnp.dot(tri, hi, preferred_element_type=F32)
                            + jnp.dot(tri, mid, preferred_element_type=F32)
                            + jnp.dot(tri, lo, preferred_element_type=F32))
    carry_ref[...] = cum[tm - 1:tm, :]
    nh = lf_ref.shape[1]
    lf_ref[...] = logf[:, :nh]
    cum_ref[...] = cum[:, :nh]
    ckm_ref[...] = jnp.where(valid, cum * LOG2E, -MASK_VALUE)[:, :nh]


def kv_proj(x, g, wk, wv, wf, bf, gk_t, bd, tri, nb, pad):
    n, d = x.shape
    tm = tri.shape[0]
    tb = n // nb
    nt = max(tb // tm, 1)
    nbg = n // (tm * nt)
    nh = FOX_HEADS
    row = lambda bi, ti: (bi * nt + ti, 0)
    col = lambda bi, ti: (bi, 0, ti)
    full = lambda bi, ti: (0, 0)
    return pl.pallas_call(
        functools.partial(_kv_kernel, pad=pad),
        grid=(nbg, nt),
        in_specs=[
            pl.BlockSpec((tm, d), row),
            pl.BlockSpec((1, d), full),
            pl.BlockSpec((d, d), full),
            pl.BlockSpec((d, d), full),
            pl.BlockSpec((d, LANES), full),
            pl.BlockSpec((1, LANES), full),
            pl.BlockSpec((1, d), full),
            pl.BlockSpec((MXU_DIM, MXU_DIM), full),
            pl.BlockSpec((tm, tm), full),
        ],
        out_specs=[
            pl.BlockSpec((1, d, tm), col),
            pl.BlockSpec((1, d, tm), col),
            pl.BlockSpec((tm, nh), row),
            pl.BlockSpec((tm, nh), row),
            pl.BlockSpec((tm, nh), row),
            pl.BlockSpec((tm, d), row),
            pl.BlockSpec((1, d, tm), col),
        ],
        out_shape=[
            jax.ShapeDtypeStruct((nbg, d, nt * tm), F32),
            jax.ShapeDtypeStruct((nbg, d, nt * tm), F32),
            jax.ShapeDtypeStruct((n, nh), F32),
            jax.ShapeDtypeStruct((n, nh), F32),
            jax.ShapeDtypeStruct((n, nh), F32),
            jax.ShapeDtypeStruct((n, d), BF16),
            jax.ShapeDtypeStruct((nbg, d, nt * tm), BF16),
        ],
        scratch_shapes=[pltpu.VMEM((1, LANES), F32)],
        compiler_params=_cparams(("parallel", "arbitrary")),
        name="kv_proj",
    )(x, g.reshape(1, d), wk, wv, wf, bf, gk_t, bd, tri)


def _fox_in_kernel(x_ref, g_ref, wq_ref, wg_ref, gq_ref, bd_ref, q_ref, gate_ref):
    xn = _rms(x_ref[...], g_ref[...]).astype(BF16)
    qraw = jnp.dot(xn, wq_ref[...], preferred_element_type=F32)
    q_ref[...] = _head_rms(qraw, bd_ref, gq_ref[...]).astype(BF16)
    gate_ref[...] = jnp.dot(xn, wg_ref[...], preferred_element_type=F32)


def fox_in(x, g, wq, wg, gq_t, bd):
    n, d = x.shape
    tm = _tile(n, 1024, 16)
    row = lambda i: (i, 0)
    full = lambda i: (0, 0)
    return pl.pallas_call(
        _fox_in_kernel,
        grid=(n // tm,),
        in_specs=[
            pl.BlockSpec((tm, d), row),
            pl.BlockSpec((1, d), full),
            pl.BlockSpec((d, d), full),
            pl.BlockSpec((d, d), full),
            pl.BlockSpec((1, d), full),
            pl.BlockSpec((MXU_DIM, MXU_DIM), full),
        ],
        out_specs=[pl.BlockSpec((tm, d), row), pl.BlockSpec((tm, d), row)],
        out_shape=[jax.ShapeDtypeStruct((n, d), BF16), jax.ShapeDtypeStruct((n, d), F32)],
        compiler_params=_cparams(("parallel",)),
        name="fox_in",
    )(x, g.reshape(1, d), wq, wg, gq_t, bd)


def _fox_prompt_kernel(q_ref, k_ref, vt_ref, g_ref, ck_ref, o_ref, m_ref, acc_ref, sa_ref, sb_ref, *, pre):
    i = pl.program_id(2)
    tq = q_ref.shape[1]
    tk = tq
    npair = vt_ref.shape[1]
    low = lax.broadcasted_iota(jnp.int32, (tq, LANES), 1) < FOX_DH
    qs = []
    for e in range(npair):
        q = q_ref[0, :, e * LANES:(e + 1) * LANES]
        zero = jnp.zeros_like(q)
        qs.append(jnp.concatenate([jnp.where(low, q, zero), jnp.where(low, zero, q)], axis=0))

    m_ref[...] = jnp.full_like(m_ref, MASK_VALUE)
    acc_ref[...] = jnp.zeros_like(acc_ref)

    def scores(e, j0, width):
        kj = k_ref[0, pl.ds(j0, width), e * LANES:(e + 1) * LANES]
        s = lax.dot_general(kj, qs[e], (((1,), (1,)), ((), ())), preferred_element_type=F32)
        ck = ck_ref[0, e, pl.ds(j0, width), :]
        return s[:, :tq] - ck[:, 0:1], s[:, tq:] - ck[:, 1:2]

    def start(b):
        return pl.multiple_of(pre + b * tk, LANES)

    def qk(b, s_ref):
        for e in range(npair):
            s0, s1 = scores(e, start(b), tk)
            s_ref[e, :, :tq] = s0
            s_ref[e, :, tq:] = s1

    def soft(e, s, vtj):
        m_old = m_ref[e]
        m_new = jnp.maximum(m_old, jnp.max(s, axis=0, keepdims=True))
        p = jnp.exp2(s - m_new).astype(BF16)
        alpha = jnp.exp2(m_old - m_new)
        vt_ones = jnp.concatenate([vtj, jnp.ones((DENOM_ROWS, vtj.shape[1]), BF16)], axis=0)
        acc_ref[e] = alpha * acc_ref[e] + jnp.dot(vt_ones, p, preferred_element_type=F32)
        m_ref[e] = m_new

    def soft_block(b, s_ref):
        for e in range(npair):
            soft(e, s_ref[e], vt_ref[0, e, :, pl.ds(start(b), tk)])

    odd = (i & 1) == 1

    @pl.when(odd)
    def _():
        qk(0, sb_ref)
        qk(1, sa_ref)
        soft_block(0, sb_ref)

    @pl.when(jnp.logical_not(odd))
    def _():
        qk(0, sa_ref)

    base = i & 1

    def body(jj, carry):
        b0 = base + 2 * jj
        qk(b0 + 1, sb_ref)
        soft_block(b0, sa_ref)
        qk(b0 + 2, sa_ref)
        soft_block(b0 + 1, sb_ref)
        return carry

    lax.fori_loop(0, i >> 1, body, 0)

    key = lax.broadcasted_iota(jnp.int32, (tk, tq), 0)
    qry = lax.broadcasted_iota(jnp.int32, (tk, tq), 1)
    ok = key <= qry
    top = lax.broadcasted_iota(jnp.int32, (LANES, tq), 0) < FOX_DH
    for e in range(npair):
        sd0 = jnp.where(ok, sa_ref[e, :, :tq], MASK_VALUE)
        sd1 = jnp.where(ok, sa_ref[e, :, tq:], MASK_VALUE)
        sp0, sp1 = scores(e, 0, pre)
        s_last = jnp.concatenate([jnp.concatenate([sp0, sd0], axis=0), jnp.concatenate([sp1, sd1], axis=0)], axis=1)
        vt_last = jnp.concatenate([vt_ref[0, e, :, 0:pre], vt_ref[0, e, :, pl.ds(start(i), tk)]], axis=1)
        soft(e, s_last, vt_last)
    for e in range(npair):
        ot = acc_ref[e, 0:LANES, :] / acc_ref[e, LANES:LANES + 1, :]
        o = jnp.where(top, ot[:, :tq], ot[:, tq:]).T
        sl = slice(e * LANES, (e + 1) * LANES)
        o_ref[0, :, sl] = (o * jax.nn.sigmoid(g_ref[0, :, sl])).astype(o_ref.dtype)


def fox_prompt_attention(q, kb, vt, gate, ck, pre):
    b, tq_all, d = q.shape
    tk_all = kb.shape[1]
    npair = ATTN_PAIRS_PER_STEP
    w = npair * LANES
    tq = _tile(tq_all, 512, LANES)
    return pl.pallas_call(
        functools.partial(_fox_prompt_kernel, pre=pre),
        grid=(b, d // w, tq_all // tq),
        in_specs=[
            pl.BlockSpec((1, tq, w), lambda bi, hi, qi: (bi, qi, hi)),
            pl.BlockSpec((1, tk_all, w), lambda bi, hi, qi: (bi, 0, hi)),
            pl.BlockSpec((1, npair, LANES, tk_all), lambda bi, hi, qi: (bi, hi, 0, 0)),
            pl.BlockSpec((1, tq, w), lambda bi, hi, qi: (bi, qi, hi)),
            pl.BlockSpec((1, npair, tk_all, 2), lambda bi, hi, qi: (bi, hi, 0, 0)),
        ],
        out_specs=pl.BlockSpec((1, tq, w), lambda bi, hi, qi: (bi, qi, hi)),
        out_shape=jax.ShapeDtypeStruct((b, tq_all, d), BF16),
        scratch_shapes=[
            pltpu.VMEM((npair, 1, 2 * tq), F32),
            pltpu.VMEM((npair, LANES + DENOM_ROWS, 2 * tq), F32),
            pltpu.VMEM((npair, tq, 2 * tq), F32),
            pltpu.VMEM((npair, tq, 2 * tq), F32),
        ],
        compiler_params=_cparams(("parallel", "parallel", "arbitrary")),
        name="fox_prompt_attention",
    )(q, kb, vt, gate, ck)


def _decode_head_mask(shape, row_axis, lane_axis):
    hrow = lax.broadcasted_iota(jnp.int32, shape, row_axis) & (FOX_HEADS - 1)
    hlane = lax.broadcasted_iota(jnp.int32, shape, lane_axis) >> 6
    return hrow == hlane


def _decode_init(q_ref, qbd_ref, m_ref, l_ref, acc_ref):
    ds, d = q_ref.shape[1:]
    q = q_ref[0].astype(F32)
    qrep = jnp.concatenate([jnp.broadcast_to(q[qi:qi + 1, :], (FOX_HEADS, d)) for qi in range(ds)], axis=0)
    qbd_ref[...] = jnp.where(_decode_head_mask(qrep.shape, 0, 1), qrep, 0.0).astype(BF16)
    m_ref[...] = jnp.full_like(m_ref, MASK_VALUE)
    l_ref[...] = jnp.zeros_like(l_ref)
    acc_ref[...] = jnp.zeros_like(acc_ref)


def _decode_update(m_ref, l_ref, acc_ref, s, v_bf, v_key_axis):
    m_old = m_ref[...]
    m_new = jnp.maximum(m_old, jnp.max(s, axis=-1, keepdims=True))
    p = jnp.exp2(s - m_new)
    alpha = jnp.exp2(m_old - m_new)
    l_ref[...] = alpha * l_ref[...] + jnp.sum(p, axis=-1, keepdims=True)
    acc_ref[...] = alpha * acc_ref[...] + lax.dot_general(
        p.astype(BF16), v_bf, (((1,), (v_key_axis,)), ((), ())), preferred_element_type=F32)
    m_ref[...] = m_new


def _decode_finish(qbd_ref, kn_ref, vn_ref, cn, g_ref, o_ref, m_ref, l_ref, acc_ref):
    ds, d = g_ref.shape[1:]
    nh = FOX_HEADS
    nr, page = ds * nh, kn_ref.shape[1]
    sn = lax.dot_general(qbd_ref[...], kn_ref[0], (((1,), (1,)), ((), ())),
                         preferred_element_type=F32) - jnp.concatenate([cn] * ds, axis=0)
    qi = lax.broadcasted_iota(jnp.int32, (nr, page), 0) >> 4
    kj = lax.broadcasted_iota(jnp.int32, (nr, page), 1)
    _decode_update(m_ref, l_ref, acc_ref, jnp.where(kj <= qi, sn, MASK_VALUE), vn_ref[0], 0)
    acc = acc_ref[...] / l_ref[...]
    acc = jnp.where(_decode_head_mask((nr, d), 0, 1), acc, 0.0)
    o = jnp.concatenate(
        [jnp.sum(acc[qi_ * nh:(qi_ + 1) * nh, :], axis=0, keepdims=True) for qi_ in range(ds)], axis=0)
    o_ref[0] = (o * jax.nn.sigmoid(g_ref[0])).astype(o_ref.dtype)


def _fox_decode_paged_kernel(pt_ref, q_ref, g_ref, kn_ref, vn_ref, cn_ref, tri_ref, *rest, pg):
    k_refs = rest[:pg]
    v_refs = rest[pg:2 * pg]
    lf_refs = rest[2 * pg:3 * pg]
    o_ref, kg_ref, vg_ref, ckg_ref, cng_ref = rest[3 * pg:3 * pg + 5]
    qbd_ref, m_ref, l_ref, acc_ref, carry_ref = rest[3 * pg + 5:]
    j = pl.program_id(1)
    nh = FOX_HEADS
    ds = q_ref.shape[1]
    page = k_refs[0].shape[2]

    @pl.when(j == 0)
    def _():
        _decode_init(q_ref, qbd_ref, m_ref, l_ref, acc_ref)
        carry_ref[...] = jnp.zeros_like(carry_ref)

    parts = []
    for pi in range(pg):
        kg_ref[0, :, pi * page:(pi + 1) * page] = k_refs[pi][0].astype(BF16)
        vg_ref[0, :, pi * page:(pi + 1) * page] = v_refs[pi][0].astype(BF16)
        parts.extend(_split3(lf_refs[pi][0]))
    w = jnp.dot(jnp.concatenate(parts, axis=0), tri_ref[...], preferred_element_type=F32)
    carry = carry_ref[...]
    cums = []
    for pi in range(pg):
        base = 3 * pi * nh
        wp = w[base:base + nh] + w[base + nh:base + 2 * nh] + w[base + 2 * nh:base + 3 * nh]
        cp = carry + wp
        cums.append(cp)
        carry = cp[:, page - 1:page]
    carry_ref[...] = carry
    ck = jnp.concatenate(cums, axis=-1) * LOG2E
    ckg_ref[0] = ck

    s = jnp.dot(qbd_ref[...], kg_ref[0], preferred_element_type=F32) - jnp.concatenate([ck] * ds, axis=0)
    _decode_update(m_ref, l_ref, acc_ref, s, vg_ref[0], 1)

    @pl.when(j == pl.num_programs(1) - 1)
    def _():
        cn = (carry_ref[...] + cn_ref[0]) * LOG2E
        cng_ref[0] = cn
        _decode_finish(qbd_ref, kn_ref, vn_ref, cn, g_ref, o_ref, m_ref, l_ref, acc_ref)


def _fox_decode_gathered_kernel(q_ref, g_ref, kn_ref, vn_ref, cng_ref, kg_ref, vg_ref, ckg_ref, o_ref,
                                qbd_ref, m_ref, l_ref, acc_ref):
    j = pl.program_id(1)
    ds = q_ref.shape[1]

    @pl.when(j == 0)
    def _():
        _decode_init(q_ref, qbd_ref, m_ref, l_ref, acc_ref)

    s = jnp.dot(qbd_ref[...], kg_ref[0], preferred_element_type=F32) - jnp.concatenate([ckg_ref[0]] * ds, axis=0)
    _decode_update(m_ref, l_ref, acc_ref, s, vg_ref[0], 1)

    @pl.when(j == pl.num_programs(1) - 1)
    def _():
        _decode_finish(qbd_ref, kn_ref, vn_ref, cng_ref[0], g_ref, o_ref, m_ref, l_ref, acc_ref)


def _decode_scratch(nr, d):
    return [pltpu.VMEM((nr, d), BF16), pltpu.VMEM((nr, 1), F32), pltpu.VMEM((nr, 1), F32), pltpu.VMEM((nr, d), F32)]


def fox_decode_attention_paged(page_table, q, gate, k_new, v_new, cum_new_t, tri_u, cache_k, cache_v, cache_lf_t):
    db, ds, d = q.shape
    npages = page_table.shape[1]
    page = cache_k.shape[2]
    pg = _tile(npages, PAGES_PER_STEP, 1)
    nh = FOX_HEADS
    past = npages * page

    def page_spec(shape, pi):
        return pl.BlockSpec(shape, lambda bi, ji, pt: (pt[bi, ji * pg + pi], 0, 0))

    per_b3 = lambda bi, ji, pt: (bi, 0, 0)
    keys3 = lambda bi, ji, pt: (bi, 0, ji)
    in_specs = [
        pl.BlockSpec((1, ds, d), per_b3),
        pl.BlockSpec((1, ds, d), per_b3),
        pl.BlockSpec((1, page, d), per_b3),
        pl.BlockSpec((1, page, d), per_b3),
        pl.BlockSpec((1, nh, page), per_b3),
        pl.BlockSpec((page, page), lambda bi, ji, pt: (0, 0)),
    ]
    in_specs += [page_spec((1, d, page), pi) for pi in range(pg)]
    in_specs += [page_spec((1, d, page), pi) for pi in range(pg)]
    in_specs += [page_spec((1, nh, page), pi) for pi in range(pg)]
    grid_spec = pltpu.PrefetchScalarGridSpec(
        num_scalar_prefetch=1,
        grid=(db, npages // pg),
        in_specs=in_specs,
        out_specs=[
            pl.BlockSpec((1, ds, d), per_b3),
            pl.BlockSpec((1, d, pg * page), keys3),
            pl.BlockSpec((1, d, pg * page), keys3),
            pl.BlockSpec((1, nh, pg * page), keys3),
            pl.BlockSpec((1, nh, page), per_b3),
        ],
        scratch_shapes=_decode_scratch(ds * nh, d) + [pltpu.VMEM((nh, 1), F32)],
    )
    return pl.pallas_call(
        functools.partial(_fox_decode_paged_kernel, pg=pg),
        grid_spec=grid_spec,
        out_shape=[
            jax.ShapeDtypeStruct((db, ds, d), F32),
            jax.ShapeDtypeStruct((db, d, past), BF16),
            jax.ShapeDtypeStruct((db, d, past), BF16),
            jax.ShapeDtypeStruct((db, nh, past), F32),
            jax.ShapeDtypeStruct((db, nh, page), F32),
        ],
        compiler_params=_cparams(("parallel", "arbitrary")),
        name="fox_decode_attention_paged",
    )(page_table, q, gate, k_new, v_new, cum_new_t, tri_u,
      *([cache_k] * pg), *([cache_v] * pg), *([cache_lf_t] * pg))


def fox_decode_attention_gathered(q, gate, k_new, v_new, cn_bias, k_g, v_g, ck_g):
    db, ds, d = q.shape
    past = k_g.shape[2]
    page = k_new.shape[1]
    nh = FOX_HEADS
    kb = _tile(past, 2 * PAGES_PER_STEP * page, LANES)
    per_b3 = lambda bi, ji: (bi, 0, 0)
    keys3 = lambda bi, ji: (bi, 0, ji)
    return pl.pallas_call(
        _fox_decode_gathered_kernel,
        grid=(db, past // kb),
        in_specs=[
            pl.BlockSpec((1, ds, d), per_b3),
            pl.BlockSpec((1, ds, d), per_b3),
            pl.BlockSpec((1, page, d), per_b3),
            pl.BlockSpec((1, page, d), per_b3),
            pl.BlockSpec((1, nh, page), per_b3),
            pl.BlockSpec((1, d, kb), keys3),
            pl.BlockSpec((1, d, kb), keys3),
            pl.BlockSpec((1, nh, kb), keys3),
        ],
        out_specs=pl.BlockSpec((1, ds, d), per_b3),
        out_shape=jax.ShapeDtypeStruct((db, ds, d), F32),
        scratch_shapes=_decode_scratch(ds * nh, d),
        compiler_params=_cparams(("parallel", "arbitrary")),
        name="fox_decode_attention_gathered",
    )(q, gate, k_new, v_new, cn_bias, k_g, v_g, ck_g)


def _block_diag_ones(n, blk):
    i = jnp.arange(n)
    return (i[:, None] // blk == i[None, :] // blk).astype(BF16)


def _lower_tri(n, seq):
    i = jnp.arange(n)
    return ((i[:, None] >= i[None, :]) & (i[:, None] // seq == i[None, :] // seq)).astype(BF16)


def _prep_weights(w_ret_in, w_ret_out, w_kvf, b_f, g_k, w_fox_qg, g_q, w_fox_out, w_mlp_up, w_mlp_down):
    d = w_kvf.shape[0]
    nh = FOX_HEADS
    wf = jnp.zeros((d, LANES), BF16).at[:, :nh].set(w_kvf[:, 2 * d:].astype(BF16))
    bf = jnp.zeros((1, LANES), F32).at[0, :nh].set(b_f)
    per_layer = lambda a: [a[l].astype(BF16) for l in range(a.shape[0])]
    return dict(
        ret_in=per_layer(w_ret_in), ret_out=per_layer(w_ret_out),
        wk=w_kvf[:, :d].astype(BF16), wv=w_kvf[:, d:2 * d].T.astype(BF16), wf=wf, bf=bf,
        gk_t=jnp.tile(g_k, nh).reshape(1, d),
        wq=per_layer(w_fox_qg[:, :, :d]), wg=per_layer(w_fox_qg[:, :, d:]),
        gq_t=(jnp.tile(g_q, (1, nh)) * (FOX_DH ** -0.5 * LOG2E)).reshape(-1, 1, d),
        fox_out=per_layer(w_fox_out), up=per_layer(w_mlp_up), down=per_layer(w_mlp_down),
        bd=_block_diag_ones(MXU_DIM, FOX_DH),
    )


def _prompt_forward(x_prompt, meta, g_attn, g_mlp, w):
    b, seq, d = x_prompt.shape
    pad = RET_CHUNK - N_META
    x = jnp.concatenate([jnp.zeros((b, pad, d), F32),
                         jnp.broadcast_to(meta[None], (b, N_META, d)), x_prompt], axis=1)
    t = x.shape[1]
    pos = jnp.arange(t) - pad
    valid = pos >= 0
    x = x.reshape(b * t, d)
    n_ret = len(w["ret_in"])
    states = []
    for l in range(n_ret):
        p = norm_matmul(x, g_attn[l], w["ret_in"][l])
        s0 = jnp.zeros((1, b, RET_HEADS, 256, 512), F32)
        og, s_new = retention(p, s0, 0, RET_CHUNK, pos, valid, BF16)
        states.append(s_new)
        x = proj_mlp(x, og, w["ret_out"][l], g_mlp[l], w["up"][l], w["down"][l])

    tm = _tile(t, 512, LANES)
    kt, vt32, logf, _, ckm, kb, vtb = kv_proj(x, w["g_kv"], w["wk"], w["wv"], w["wf"], w["bf"], w["gk_t"],
                                              w["bd"], _lower_tri(tm, tm), b, pad)
    pre = RET_CHUNK
    nh = FOX_HEADS
    ck = ckm.reshape(b, t, nh // 2, 2).transpose(0, 2, 1, 3)
    kb3 = kb.reshape(b, t, d)
    vt = vtb.reshape(b, d // LANES, LANES, t)
    xr = x.reshape(b, t, d)[:, pre:].reshape(b * seq, d)
    for l in range(n_ret, g_attn.shape[0]):
        q, gate = fox_in(xr, g_attn[l], w["wq"][l - n_ret], w["wg"][l - n_ret], w["gq_t"][l - n_ret], w["bd"])
        a = fox_prompt_attention(q.reshape(b, seq, d), kb3, vt, gate.reshape(b, seq, d), ck, pre)
        xr = proj_mlp(xr, a.reshape(b * seq, d), w["fox_out"][l - n_ret], g_mlp[l], w["up"][l], w["down"][l])
    y = xr.reshape(b, seq, d)
    k4 = kt.reshape(b, nh, FOX_DH, t)[:, :, :, pad:].transpose(0, 3, 1, 2)
    v4 = vt32.reshape(b, nh, FOX_DH, t)[:, :, :, pad:].transpose(0, 3, 1, 2)
    return y, jnp.stack(states), k4, v4, logf.reshape(b, t, nh)[:, pad:]


def _sample_forward(x_sample, state_ret, cache_k, cache_v, cache_logf, page_table, g_attn, g_mlp, w):
    db, ds, d = x_sample.shape
    n_pool, page, nh, dh = cache_k.shape
    past = page_table.shape[1] * page
    pos = past + jnp.arange(ds)
    valid = jnp.ones((ds,), bool)
    x = x_sample.reshape(db * ds, d)
    n_ret = len(w["ret_in"])
    states = []
    for l in range(n_ret):
        p = norm_matmul(x, g_attn[l], w["ret_in"][l])
        og, s_new = retention(p, state_ret, l, ds, pos, valid, F32)
        states.append(s_new)
        x = proj_mlp(x, og, w["ret_out"][l], g_mlp[l], w["up"][l], w["down"][l])

    n = db * ds
    kt, vt32, logf, cum, _, kb, vtb = kv_proj(x, w["g_kv"], w["wk"], w["wv"], w["wf"], w["bf"], w["gk_t"],
                                              w["bd"], _lower_tri(n, ds), 1, 0)
    k, v = kt[0].T, vt32[0].T
    zrows = jnp.zeros((db, page - ds, d), BF16)
    k_new = jnp.concatenate([kb.reshape(db, ds, d), zrows], axis=1)
    v_new = jnp.concatenate([vtb[0].T.reshape(db, ds, d), zrows], axis=1)
    cum_t = jnp.zeros((db, nh, page), F32).at[:, :, :ds].set(cum.reshape(db, ds, nh).transpose(0, 2, 1))
    tri_u = _lower_tri(page, page).T
    ck3 = cache_k.transpose(0, 2, 3, 1).reshape(n_pool, d, page)
    cv3 = cache_v.transpose(0, 2, 3, 1).reshape(n_pool, d, page)
    clf_t = cache_logf.transpose(0, 2, 1)
    for l in range(n_ret, g_attn.shape[0]):
        q, gate = fox_in(x, g_attn[l], w["wq"][l - n_ret], w["wg"][l - n_ret], w["gq_t"][l - n_ret], w["bd"])
        q3, gate3 = q.reshape(db, ds, d), gate.reshape(db, ds, d)
        if l == n_ret:
            a, k_g, v_g, ck_g, cn_bias = fox_decode_attention_paged(
                page_table, q3, gate3, k_new, v_new, cum_t, tri_u, ck3, cv3, clf_t)
        else:
            a = fox_decode_attention_gathered(q3, gate3, k_new, v_new, cn_bias, k_g, v_g, ck_g)
        x = proj_mlp(x, a.reshape(n, d), w["fox_out"][l - n_ret], g_mlp[l], w["up"][l], w["down"][l])
    return (x.reshape(db, ds, d), jnp.stack(states), k.reshape(db, ds, nh, dh), v.reshape(db, ds, nh, dh),
            logf.reshape(db, ds, nh))


def kernel(x_prompt, x_sample, state_ret, cache_k, cache_v, cache_logf, page_table, meta, g_attn, g_mlp,
           w_ret_in, w_ret_out, g_kv, w_kvf, b_f, g_k, w_fox_qg, g_q, w_fox_out, w_mlp_up, w_mlp_down):
    w = _prep_weights(w_ret_in, w_ret_out, w_kvf, b_f, g_k, w_fox_qg, g_q, w_fox_out, w_mlp_up, w_mlp_down)
    w["g_kv"] = g_kv
    y_p, s_p, k_p, v_p, lf_p = _prompt_forward(x_prompt, meta, g_attn, g_mlp, w)
    y_s, s_s, k_s, v_s, lf_s = _sample_forward(x_sample, state_ret, cache_k, cache_v, cache_logf, page_table,
                                               g_attn, g_mlp, w)
    return (y_p, y_s, s_p, s_s, k_p, v_p, lf_p, k_s, v_s, lf_s)
```

```python
import functools

import jax
import jax.numpy as jnp
from jax import lax
from jax.experimental import pallas as pl
from jax.experimental.pallas import tpu as pltpu

F32 = jnp.float32
BF16 = jnp.bfloat16

N_META = 16
RET_HEADS = 4
RET_CHUNK = 128
ROPE_BASE = 10000.0
FOX_HEADS = 16
FOX_DH = 64
EPS = 1e-6
MASK_VALUE = -1e30

LANES = 128
MXU_DIM = 256
VMEM_LIMIT = 56 * 1024 * 1024
PAGES_PER_STEP = 8
DENOM_ROWS = 16
ATTN_PAIRS_PER_STEP = 2
LOG2E = 1.4426950408889634


def _cparams(sem):
    return pltpu.CompilerParams(dimension_semantics=sem, vmem_limit_bytes=VMEM_LIMIT)


def _tile(n, target, mult=8):
    best = None
    for t in range(mult, min(n, target) + 1, mult):
        if n % t == 0:
            best = t
    assert best is not None, (n, target, mult)
    return best


def _rms(x, g_row):
    ms = jnp.mean(x * x, axis=-1, keepdims=True)
    return x * lax.rsqrt(ms + EPS) * g_row


def _split2(x):
    hi = x.astype(BF16)
    lo = (x - hi.astype(F32)).astype(BF16)
    return hi, lo


def _split3(x):
    hi = x.astype(BF16)
    r = x - hi.astype(F32)
    mid = r.astype(BF16)
    lo = (r - mid.astype(F32)).astype(BF16)
    return hi, mid, lo


def _head_rms(x, bd_ref, g_row):
    xx = x * x
    hi, lo = _split2(xx)
    bd = bd_ref[...]
    parts = []
    for c in range(x.shape[1] // MXU_DIM):
        sl = slice(c * MXU_DIM, (c + 1) * MXU_DIM)
        parts.append(jnp.dot(hi[:, sl], bd, preferred_element_type=F32)
                     + jnp.dot(lo[:, sl], bd, preferred_element_type=F32))
    ss = jnp.concatenate(parts, axis=-1)
    return x * lax.rsqrt(ss * (1.0 / FOX_DH) + EPS) * g_row


def _norm_matmul_kernel(x_ref, g_ref, w_ref, o_ref, xn_ref):
    @pl.when(pl.program_id(1) == 0)
    def _():
        xn_ref[...] = _rms(x_ref[...], g_ref[...]).astype(BF16)

    o_ref[...] = jnp.dot(xn_ref[...], w_ref[...], preferred_element_type=F32).astype(o_ref.dtype)


def norm_matmul(x, g, w, out_dtype):
    n, d = x.shape
    nout = w.shape[1]
    tm = _tile(n, 1536)
    tn = _tile(nout, 1024, LANES)
    return pl.pallas_call(
        _norm_matmul_kernel,
        grid=(n // tm, nout // tn),
        in_specs=[
            pl.BlockSpec((tm, d), lambda i, j: (i, 0)),
            pl.BlockSpec((1, d), lambda i, j: (0, 0)),
            pl.BlockSpec((d, tn), lambda i, j: (0, j)),
        ],
        out_specs=pl.BlockSpec((tm, tn), lambda i, j: (i, j)),
        out_shape=jax.ShapeDtypeStruct((n, nout), out_dtype),
        scratch_shapes=[pltpu.VMEM((tm, d), BF16)],
        compiler_params=_cparams(("parallel", "arbitrary")),
        name="norm_matmul",
    )(x, g.reshape(1, d), w)


def _retention_kernel(q_ref, k_ref, v_ref, gt_ref, cq_ref, sq_ref, ck_ref, sk_ref,
                      dm_ref, qd_ref, kd_ref, gc_ref, s0_ref, og_ref, s_ref):
    @pl.when(pl.program_id(1) == 0)
    def _():
        s_ref[0] = s0_ref[0, 0]

    nh, dk, dv = s_ref.shape[1:]
    half = dk // 2
    cq, sq = cq_ref[...], sq_ref[...]
    ck, sk = ck_ref[...], sk_ref[...]
    for h in range(nh):
        q1 = q_ref[:, h * dk:h * dk + half].astype(F32)
        q2 = q_ref[:, h * dk + half:(h + 1) * dk].astype(F32)
        k1 = k_ref[:, h * dk:h * dk + half].astype(F32)
        k2 = k_ref[:, h * dk + half:(h + 1) * dk].astype(F32)
        qr = jnp.concatenate([q1 * cq - q2 * sq, q2 * cq + q1 * sq], axis=-1)
        kr = jnp.concatenate([k1 * ck - k2 * sk, k2 * ck + k1 * sk], axis=-1)
        qb = qr.astype(BF16)
        kb = kr.astype(BF16)
        vb = v_ref[:, h * dv:(h + 1) * dv].astype(BF16)
        s_old = s_ref[0, h]

        scores = lax.dot_general(qb, kb, (((1,), (1,)), ((), ())),
                                 preferred_element_type=F32) * dm_ref[h]
        intra = jnp.dot(scores.astype(BF16), vb, preferred_element_type=F32)
        cross = jnp.dot(qb, s_old.astype(BF16), preferred_element_type=F32) * qd_ref[h]
        o = intra + cross

        kd = (kr * kd_ref[h]).astype(BF16)
        s_ref[0, h] = gc_ref[h] * s_old + lax.dot_general(
            kd, vb, (((0,), (0,)), ((), ())), preferred_element_type=F32)

        on = o * lax.rsqrt(jnp.mean(o * o, axis=-1, keepdims=True) + EPS)
        gt = gt_ref[:, h * dv:(h + 1) * dv].astype(F32)
        og_ref[:, h * dv:(h + 1) * dv] = (gt * jax.nn.sigmoid(gt) * on).astype(og_ref.dtype)


def _retention_tables(chunk, pos, valid):
    dk = 256
    half = dk // 2
    lg = jnp.log1p(-jnp.exp2(-5.0 - jnp.arange(RET_HEADS, dtype=F32)))
    idx = jnp.arange(chunk, dtype=F32)
    diff = idx[:, None] - idx[None, :]
    dmat = jnp.where(diff >= 0, jnp.exp(lg[:, None, None] * jnp.maximum(diff, 0.0)), 0.0)
    qdec = jnp.exp(lg[:, None] * (idx[None, :] + 1.0))[:, :, None]
    kdec = jnp.exp(lg[:, None] * (chunk - 1.0 - idx[None, :]))[:, :, None]
    gc = jnp.exp(lg * chunk)[:, None, None]
    inv_freq = ROPE_BASE ** (-jnp.arange(half, dtype=F32) / half)
    ang = pos.astype(F32)[:, None] * inv_freq[None, :]
    cos, sin = jnp.cos(ang), jnp.sin(ang)
    kscale = (dk ** -0.5) * valid.astype(F32)[:, None]
    return dmat, qdec, kdec, gc, cos, sin, cos * kscale, sin * kscale


def retention(p, s0_all, layer, chunk, pos, valid, out_dtype):
    _, b, h, dk, dv = s0_all.shape
    n = p.shape[0]
    t = n // b
    nc = t // chunk
    dmat, qdec, kdec, gc, cq, sq, ck, sk = _retention_tables(chunk, pos, valid)
    row = lambda bi, ci: bi * nc + ci
    kblk = (h * dk) // (h * dk)
    vblk = (2 * h * dk) // (h * dv)
    gblk = (2 * h * dk + h * dv) // (h * dv)
    rope_spec = pl.BlockSpec((chunk, dk // 2), lambda bi, ci: (ci, 0))
    full3 = lambda bi, ci: (0, 0, 0)
    return pl.pallas_call(
        _retention_kernel,
        grid=(b, nc),
        in_specs=[
            pl.BlockSpec((chunk, h * dk), lambda bi, ci: (row(bi, ci), 0)),
            pl.BlockSpec((chunk, h * dk), lambda bi, ci: (row(bi, ci), kblk)),
            pl.BlockSpec((chunk, h * dv), lambda bi, ci: (row(bi, ci), vblk)),
            pl.BlockSpec((chunk, h * dv), lambda bi, ci: (row(bi, ci), gblk)),
            rope_spec, rope_spec, rope_spec, rope_spec,
            pl.BlockSpec((h, chunk, chunk), full3),
            pl.BlockSpec((h, chunk, 1), full3),
            pl.BlockSpec((h, chunk, 1), full3),
            pl.BlockSpec((h, 1, 1), full3),
            pl.BlockSpec((1, 1, h, dk, dv), lambda bi, ci: (layer, bi, 0, 0, 0)),
        ],
        out_specs=[
            pl.BlockSpec((chunk, h * dv), lambda bi, ci: (row(bi, ci), 0)),
            pl.BlockSpec((1, h, dk, dv), lambda bi, ci: (bi, 0, 0, 0)),
        ],
        out_shape=[
            jax.ShapeDtypeStruct((n, h * dv), out_dtype),
            jax.ShapeDtypeStruct((b, h, dk, dv), F32),
        ],
        compiler_params=_cparams(("parallel", "arbitrary")),
        name="retention",
    )(p, p, p, p, cq, sq, ck, sk, dmat, qdec, kdec, gc, s0_all)


def _proj_mlp_kernel(x_ref, a_ref, wo_ref, g_ref, wu_ref, wd_ref, o_ref, xn_ref):
    @pl.when(pl.program_id(1) == 0)
    def _():
        x1 = x_ref[...] + jnp.dot(a_ref[...].astype(BF16), wo_ref[...], preferred_element_type=F32)
        o_ref[...] = x1
        xn_ref[...] = _rms(x1, g_ref[...]).astype(BF16)

    hdn = jnp.dot(xn_ref[...], wu_ref[...], preferred_element_type=F32)
    hdn = jnp.square(jnp.maximum(hdn, 0.0)).astype(BF16)
    o_ref[...] += jnp.dot(hdn, wd_ref[...], preferred_element_type=F32)


def proj_mlp(x, a, wo, g, wu, wd):
    n, d = x.shape
    ka = a.shape[1]
    dff = wu.shape[1]
    tm = _tile(n, 768, 16)
    tf = _tile(dff, 1024, LANES)
    return pl.pallas_call(
        _proj_mlp_kernel,
        grid=(n // tm, dff // tf),
        in_specs=[
            pl.BlockSpec((tm, d), lambda i, f: (i, 0)),
            pl.BlockSpec((tm, ka), lambda i, f: (i, 0)),
            pl.BlockSpec((ka, d), lambda i, f: (0, 0)),
            pl.BlockSpec((1, d), lambda i, f: (0, 0)),
            pl.BlockSpec((d, tf), lambda i, f: (0, f)),
            pl.BlockSpec((tf, d), lambda i, f: (f, 0)),
        ],
        out_specs=pl.BlockSpec((tm, d), lambda i, f: (i, 0)),
        out_shape=jax.ShapeDtypeStruct((n, d), F32),
        scratch_shapes=[pltpu.VMEM((tm, d), BF16)],
        compiler_params=_cparams(("parallel", "arbitrary")),
        name="proj_mlp",
    )(x, a, wo, g.reshape(1, d), wu, wd)


def _kv_kernel(x_ref, g_ref, wk_ref, wv_ref, wf_ref, bf_ref, gk_ref, bd_ref, tri_ref,
               kt_ref, vt_ref, lf_ref, cum_ref, ckm_ref, kb_ref, vtb_ref, carry_ref, *, pad):
    t = pl.program_id(1)
    tm = x_ref.shape[0]

    @pl.when(t == 0)
    def _():
        carry_ref[...] = jnp.zeros_like(carry_ref)

    xn = _rms(x_ref[...], g_ref[...]).astype(BF16)
    kraw = jnp.dot(xn, wk_ref[...], preferred_element_type=F32)
    k = _head_rms(kraw, bd_ref, gk_ref[...])
    vt = lax.dot_general(wv_ref[...], xn, (((1,), (1,)), ((), ())), preferred_element_type=F32)
    kt_ref[0] = k.T
    vt_ref[0] = vt
    kb_ref[...] = k.astype(BF16)
    vtb_ref[0] = vt.astype(BF16)

    z = jnp.dot(xn, wf_ref[...], preferred_element_type=F32) + bf_ref[...]
    logf = jnp.minimum(z, 0.0) - jnp.log1p(jnp.exp(-jnp.abs(z)))
    rows = t * tm + lax.broadcasted_iota(jnp.int32, (tm, LANES), 0)
    valid = rows >= pad
    logf = jnp.where(valid, logf, 0.0)
    hi, mid, lo = _split3(logf)
    tri = tri_ref[...]
    cum = carry_ref[...] + (jnp.dot(tri, hi, preferred_element_type=F32)
                            + jnp.dot(tri, mid, preferred_element_type=F32)
                            + jnp.dot(tri, lo, preferred_element_type=F32))
    carry_ref[...] = cum[tm - 1:tm, :]
    nh = lf_ref.shape[1]
    lf_ref[...] = logf[:, :nh]
    cum_ref[...] = cum[:, :nh]
    ckm_ref[...] = jnp.where(valid, cum * LOG2E, -MASK_VALUE)[:, :nh]


def kv_proj(x, g, wk, wv, wf, bf, gk_t, bd, tri, nb, pad):
    n, d = x.shape
    tm = tri.shape[0]
    tb = n // nb
    nt = max(tb // tm, 1)
    nbg = n // (tm * nt)
    nh = FOX_HEADS
    row = lambda bi, ti: (bi * nt + ti, 0)
    col = lambda bi, ti: (bi, 0, ti)
    full = lambda bi, ti: (0, 0)
    return pl.pallas_call(
        functools.partial(_kv_kernel, pad=pad),
        grid=(nbg, nt),
        in_specs=[
            pl.BlockSpec((tm, d), row),
            pl.BlockSpec((1, d), full),
            pl.BlockSpec((d, d), full),
            pl.BlockSpec((d, d), full),
            pl.BlockSpec((d, LANES), full),
            pl.BlockSpec((1, LANES), full),
            pl.BlockSpec((1, d), full),
            pl.BlockSpec((MXU_DIM, MXU_DIM), full),
            pl.BlockSpec((tm, tm), full),
        ],
        out_specs=[
            pl.BlockSpec((1, d, tm), col),
            pl.BlockSpec((1, d, tm), col),
            pl.BlockSpec((tm, nh), row),
            pl.BlockSpec((tm, nh), row),
            pl.BlockSpec((tm, nh), row),
            pl.BlockSpec((tm, d), row),
            pl.BlockSpec((1, d, tm), col),
        ],
        out_shape=[
            jax.ShapeDtypeStruct((nbg, d, nt * tm), F32),
            jax.ShapeDtypeStruct((nbg, d, nt * tm), F32),
            jax.ShapeDtypeStruct((n, nh), F32),
            jax.ShapeDtypeStruct((n, nh), F32),
            jax.ShapeDtypeStruct((n, nh), F32),
            jax.ShapeDtypeStruct((n, d), BF16),
            jax.ShapeDtypeStruct((nbg, d, nt * tm), BF16),
        ],
        scratch_shapes=[pltpu.VMEM((1, LANES), F32)],
        compiler_params=_cparams(("parallel", "arbitrary")),
        name="kv_proj",
    )(x, g.reshape(1, d), wk, wv, wf, bf, gk_t, bd, tri)


def _fox_in_kernel(x_ref, g_ref, wq_ref, wg_ref, gq_ref, bd_ref, q_ref, gate_ref):
    xn = _rms(x_ref[...], g_ref[...]).astype(BF16)
    qraw = jnp.dot(xn, wq_ref[...], preferred_element_type=F32)
    q_ref[...] = _head_rms(qraw, bd_ref, gq_ref[...]).astype(BF16)
    gate_ref[...] = jnp.dot(xn, wg_ref[...], preferred_element_type=F32).astype(gate_ref.dtype)


def fox_in(x, g, wq, wg, gq_t, bd):
    n, d = x.shape
    tm = _tile(n, 1024, 16)
    row = lambda i: (i, 0)
    full = lambda i: (0, 0)
    return pl.pallas_call(
        _fox_in_kernel,
        grid=(n // tm,),
        in_specs=[
            pl.BlockSpec((tm, d), row),
            pl.BlockSpec((1, d), full),
            pl.BlockSpec((d, d), full),
            pl.BlockSpec((d, d), full),
            pl.BlockSpec((1, d), full),
            pl.BlockSpec((MXU_DIM, MXU_DIM), full),
        ],
        out_specs=[pl.BlockSpec((tm, d), row), pl.BlockSpec((tm, d), row)],
        out_shape=[jax.ShapeDtypeStruct((n, d), BF16), jax.ShapeDtypeStruct((n, d), BF16)],
        compiler_params=_cparams(("parallel",)),
        name="fox_in",
    )(x, g.reshape(1, d), wq, wg, gq_t, bd)


def _fox_prompt_kernel(q_ref, k_ref, vt_ref, g_ref, ck_ref, o_ref, m_ref, acc_ref, sa_ref, sb_ref, *, pre):
    i = pl.program_id(2)
    tq = q_ref.shape[1]
    tk = tq
    npair = vt_ref.shape[1]
    low = lax.broadcasted_iota(jnp.int32, (tq, LANES), 1) < FOX_DH
    qs = []
    for e in range(npair):
        q = q_ref[0, :, e * LANES:(e + 1) * LANES]
        zero = jnp.zeros_like(q)
        qs.append(jnp.concatenate([jnp.where(low, q, zero), jnp.where(low, zero, q)], axis=0))

    m_ref[...] = jnp.full_like(m_ref, MASK_VALUE)
    acc_ref[...] = jnp.zeros_like(acc_ref)

    def scores(e, j0, width):
        kj = k_ref[0, pl.ds(j0, width), e * LANES:(e + 1) * LANES]
        s = lax.dot_general(kj, qs[e], (((1,), (1,)), ((), ())), preferred_element_type=F32)
        ck = ck_ref[0, e, pl.ds(j0, width), :]
        return s[:, :tq] - ck[:, 0:1], s[:, tq:] - ck[:, 1:2]

    def start(b):
        return pl.multiple_of(pre + b * tk, LANES)

    def qk(b, s_ref):
        for e in range(npair):
            s0, s1 = scores(e, start(b), tk)
            s_ref[e, :, :tq] = s0
            s_ref[e, :, tq:] = s1

    def soft(e, s, vtj):
        m_old = m_ref[e]
        m_new = jnp.maximum(m_old, jnp.max(s, axis=0, keepdims=True))
        p = jnp.exp2(s - m_new).astype(BF16)
        alpha = jnp.exp2(m_old - m_new)
        vt_ones = jnp.concatenate([vtj, jnp.ones((DENOM_ROWS, vtj.shape[1]), BF16)], axis=0)
        acc_ref[e] = alpha * acc_ref[e] + jnp.dot(vt_ones, p, preferred_element_type=F32)
        m_ref[e] = m_new

    def soft_block(b, s_ref):
        for e in range(npair):
            soft(e, s_ref[e], vt_ref[0, e, :, pl.ds(start(b), tk)])

    odd = (i & 1) == 1

    @pl.when(odd)
    def _():
        qk(0, sb_ref)
        qk(1, sa_ref)
        soft_block(0, sb_ref)

    @pl.when(jnp.logical_not(odd))
    def _():
        qk(0, sa_ref)

    base = i & 1

    def body(jj, carry):
        b0 = base + 2 * jj
        qk(b0 + 1, sb_ref)
        soft_block(b0, sa_ref)
        qk(b0 + 2, sa_ref)
        soft_block(b0 + 1, sb_ref)
        return carry

    lax.fori_loop(0, i >> 1, body, 0)

    key = lax.broadcasted_iota(jnp.int32, (tk, tq), 0)
    qry = lax.broadcasted_iota(jnp.int32, (tk, tq), 1)
    ok = key <= qry
    top = lax.broadcasted_iota(jnp.int32, (LANES, tq), 0) < FOX_DH
    for e in range(npair):
        sd0 = jnp.where(ok, sa_ref[e, :, :tq], MASK_VALUE)
        sd1 = jnp.where(ok, sa_ref[e, :, tq:], MASK_VALUE)
        sp0, sp1 = scores(e, 0, pre)
        s_last = jnp.concatenate([jnp.concatenate([sp0, sd0], axis=0), jnp.concatenate([sp1, sd1], axis=0)], axis=1)
        vt_last = jnp.concatenate([vt_ref[0, e, :, 0:pre], vt_ref[0, e, :, pl.ds(start(i), tk)]], axis=1)
        soft(e, s_last, vt_last)
    for e in range(npair):
        ot = acc_ref[e, 0:LANES, :] / acc_ref[e, LANES:LANES + 1, :]
        o = jnp.where(top, ot[:, :tq], ot[:, tq:]).T
        sl = slice(e * LANES, (e + 1) * LANES)
        o_ref[0, :, sl] = (o * jax.nn.sigmoid(g_ref[0, :, sl].astype(F32))).astype(o_ref.dtype)


def fox_prompt_attention(q, kb, vt, gate, ck, pre):
    b, tq_all, d = q.shape
    tk_all = kb.shape[1]
    npair = ATTN_PAIRS_PER_STEP
    w = npair * LANES
    tq = _tile(tq_all, 512, LANES)
    return pl.pallas_call(
        functools.partial(_fox_prompt_kernel, pre=pre),
        grid=(b, d // w, tq_all // tq),
        in_specs=[
            pl.BlockSpec((1, tq, w), lambda bi, hi, qi: (bi, qi, hi)),
            pl.BlockSpec((1, tk_all, w), lambda bi, hi, qi: (bi, 0, hi)),
            pl.BlockSpec((1, npair, LANES, tk_all), lambda bi, hi, qi: (bi, hi, 0, 0)),
            pl.BlockSpec((1, tq, w), lambda bi, hi, qi: (bi, qi, hi)),
            pl.BlockSpec((1, npair, tk_all, 2), lambda bi, hi, qi: (bi, hi, 0, 0)),
        ],
        out_specs=pl.BlockSpec((1, tq, w), lambda bi, hi, qi: (bi, qi, hi)),
        out_shape=jax.ShapeDtypeStruct((b, tq_all, d), BF16),
        scratch_shapes=[
            pltpu.VMEM((npair, 1, 2 * tq), F32),
            pltpu.VMEM((npair, LANES + DENOM_ROWS, 2 * tq), F32),
            pltpu.VMEM((npair, tq, 2 * tq), F32),
            pltpu.VMEM((npair, tq, 2 * tq), F32),
        ],
        compiler_params=_cparams(("parallel", "parallel", "arbitrary")),
        name="fox_prompt_attention",
    )(q, kb, vt, gate, ck)


def _decode_head_mask(shape, row_axis, lane_axis):
    hrow = lax.broadcasted_iota(jnp.int32, shape, row_axis) & (FOX_HEADS - 1)
    hlane = lax.broadcasted_iota(jnp.int32, shape, lane_axis) >> 6
    return hrow == hlane


def _decode_init(q_ref, qbd_ref, m_ref, l_ref, acc_ref):
    ds, d = q_ref.shape[1:]
    q = q_ref[0].astype(F32)
    qrep = jnp.concatenate([jnp.broadcast_to(q[qi:qi + 1, :], (FOX_HEADS, d)) for qi in range(ds)], axis=0)
    qbd_ref[...] = jnp.where(_decode_head_mask(qrep.shape, 0, 1), qrep, 0.0).astype(BF16)
    m_ref[...] = jnp.full_like(m_ref, MASK_VALUE)
    l_ref[...] = jnp.zeros_like(l_ref)
    acc_ref[...] = jnp.zeros_like(acc_ref)


def _decode_update(m_ref, l_ref, acc_ref, s, v_bf, v_key_axis):
    m_old = m_ref[...]
    m_new = jnp.maximum(m_old, jnp.max(s, axis=-1, keepdims=True))
    p = jnp.exp2(s - m_new)
    alpha = jnp.exp2(m_old - m_new)
    l_ref[...] = alpha * l_ref[...] + jnp.sum(p, axis=-1, keepdims=True)
    acc_ref[...] = alpha * acc_ref[...] + lax.dot_general(
        p.astype(BF16), v_bf, (((1,), (v_key_axis,)), ((), ())), preferred_element_type=F32)
    m_ref[...] = m_new


def _decode_finish(qbd_ref, kn_ref, vn_ref, cn, g_ref, o_ref, m_ref, l_ref, acc_ref):
    ds, d = g_ref.shape[1:]
    nh = FOX_HEADS
    nr, page = ds * nh, kn_ref.shape[1]
    sn = lax.dot_general(qbd_ref[...], kn_ref[0], (((1,), (1,)), ((), ())),
                         preferred_element_type=F32) - jnp.concatenate([cn] * ds, axis=0)
    qi = lax.broadcasted_iota(jnp.int32, (nr, page), 0) >> 4
    kj = lax.broadcasted_iota(jnp.int32, (nr, page), 1)
    _decode_update(m_ref, l_ref, acc_ref, jnp.where(kj <= qi, sn, MASK_VALUE), vn_ref[0], 0)
    acc = acc_ref[...] / l_ref[...]
    acc = jnp.where(_decode_head_mask((nr, d), 0, 1), acc, 0.0)
    o = jnp.concatenate(
        [jnp.sum(acc[qi_ * nh:(qi_ + 1) * nh, :], axis=0, keepdims=True) for qi_ in range(ds)], axis=0)
    o_ref[0] = (o * jax.nn.sigmoid(g_ref[0].astype(F32))).astype(o_ref.dtype)


def _fox_decode_paged_kernel(pt_ref, q_ref, g_ref, kn_ref, vn_ref, cn_ref, tri_ref, *rest, pg):
    k_refs = rest[:pg]
    v_refs = rest[pg:2 * pg]
    lf_refs = rest[2 * pg:3 * pg]
    o_ref, kg_ref, vg_ref, ckg_ref, cng_ref = rest[3 * pg:3 * pg + 5]
    qbd_ref, m_ref, l_ref, acc_ref, carry_ref = rest[3 * pg + 5:]
    j = pl.program_id(1)
    nh = FOX_HEADS
    ds = q_ref.shape[1]
    page = k_refs[0].shape[2]

    @pl.when(j == 0)
    def _():
        _decode_init(q_ref, qbd_ref, m_ref, l_ref, acc_ref)
        carry_ref[...] = jnp.zeros_like(carry_ref)

    parts = []
    for pi in range(pg):
        kg_ref[0, :, pi * page:(pi + 1) * page] = k_refs[pi][0].astype(BF16)
        vg_ref[0, :, pi * page:(pi + 1) * page] = v_refs[pi][0].astype(BF16)
        parts.extend(_split3(lf_refs[pi][0]))
    w = jnp.dot(jnp.concatenate(parts, axis=0), tri_ref[...], preferred_element_type=F32)
    carry = carry_ref[...]
    cums = []
    for pi in range(pg):
        base = 3 * pi * nh
        wp = w[base:base + nh] + w[base + nh:base + 2 * nh] + w[base + 2 * nh:base + 3 * nh]
        cp = carry + wp
        cums.append(cp)
        carry = cp[:, page - 1:page]
    carry_ref[...] = carry
    ck = jnp.concatenate(cums, axis=-1) * LOG2E
    ckg_ref[0] = ck

    s = jnp.dot(qbd_ref[...], kg_ref[0], preferred_element_type=F32) - jnp.concatenate([ck] * ds, axis=0)
    _decode_update(m_ref, l_ref, acc_ref, s, vg_ref[0], 1)

    @pl.when(j == pl.num_programs(1) - 1)
    def _():
        cn = (carry_ref[...] + cn_ref[0]) * LOG2E
        cng_ref[0] = cn
        _decode_finish(qbd_ref, kn_ref, vn_ref, cn, g_ref, o_ref, m_ref, l_ref, acc_ref)


def _fox_decode_gathered_kernel(q_ref, g_ref, kn_ref, vn_ref, cng_ref, kg_ref, vg_ref, ckg_ref, o_ref,
                                qbd_ref, m_ref, l_ref, acc_ref):
    j = pl.program_id(1)
    ds = q_ref.shape[1]

    @pl.when(j == 0)
    def _():
        _decode_init(q_ref, qbd_ref, m_ref, l_ref, acc_ref)

    s = jnp.dot(qbd_ref[...], kg_ref[0], preferred_element_type=F32) - jnp.concatenate([ckg_ref[0]] * ds, axis=0)
    _decode_update(m_ref, l_ref, acc_ref, s, vg_ref[0], 1)

    @pl.when(j == pl.num_programs(1) - 1)
    def _():
        _decode_finish(qbd_ref, kn_ref, vn_ref, cng_ref[0], g_ref, o_ref, m_ref, l_ref, acc_ref)


def _decode_scratch(nr, d):
    return [pltpu.VMEM((nr, d), BF16), pltpu.VMEM((nr, 1), F32), pltpu.VMEM((nr, 1), F32), pltpu.VMEM((nr, d), F32)]


def fox_decode_attention_paged(page_table, q, gate, k_new, v_new, cum_new_t, tri_u, cache_k, cache_v, cache_lf_t):
    db, ds, d = q.shape
    npages = page_table.shape[1]
    page = cache_k.shape[2]
    pg = _tile(npages, PAGES_PER_STEP, 1)
    nh = FOX_HEADS
    past = npages * page

    def page_spec(shape, pi):
        return pl.BlockSpec(shape, lambda bi, ji, pt: (pt[bi, ji * pg + pi], 0, 0))

    per_b3 = lambda bi, ji, pt: (bi, 0, 0)
    keys3 = lambda bi, ji, pt: (bi, 0, ji)
    in_specs = [
        pl.BlockSpec((1, ds, d), per_b3),
        pl.BlockSpec((1, ds, d), per_b3),
        pl.BlockSpec((1, page, d), per_b3),
        pl.BlockSpec((1, page, d), per_b3),
        pl.BlockSpec((1, nh, page), per_b3),
        pl.BlockSpec((page, page), lambda bi, ji, pt: (0, 0)),
    ]
    in_specs += [page_spec((1, d, page), pi) for pi in range(pg)]
    in_specs += [page_spec((1, d, page), pi) for pi in range(pg)]
    in_specs += [page_spec((1, nh, page), pi) for pi in range(pg)]
    grid_spec = pltpu.PrefetchScalarGridSpec(
        num_scalar_prefetch=1,
        grid=(db, npages // pg),
        in_specs=in_specs,
        out_specs=[
            pl.BlockSpec((1, ds, d), per_b3),
            pl.BlockSpec((1, d, pg * page), keys3),
            pl.BlockSpec((1, d, pg * page), keys3),
            pl.BlockSpec((1, nh, pg * page), keys3),
            pl.BlockSpec((1, nh, page), per_b3),
        ],
        scratch_shapes=_decode_scratch(ds * nh, d) + [pltpu.VMEM((nh, 1), F32)],
    )
    return pl.pallas_call(
        functools.partial(_fox_decode_paged_kernel, pg=pg),
        grid_spec=grid_spec,
        out_shape=[
            jax.ShapeDtypeStruct((db, ds, d), F32),
            jax.ShapeDtypeStruct((db, d, past), BF16),
            jax.ShapeDtypeStruct((db, d, past), BF16),
            jax.ShapeDtypeStruct((db, nh, past), F32),
            jax.ShapeDtypeStruct((db, nh, page), F32),
        ],
        compiler_params=_cparams(("parallel", "arbitrary")),
        name="fox_decode_attention_paged",
    )(page_table, q, gate, k_new, v_new, cum_new_t, tri_u,
      *([cache_k] * pg), *([cache_v] * pg), *([cache_lf_t] * pg))


def fox_decode_attention_gathered(q, gate, k_new, v_new, cn_bias, k_g, v_g, ck_g):
    db, ds, d = q.shape
    past = k_g.shape[2]
    page = k_new.shape[1]
    nh = FOX_HEADS
    kb = _tile(past, 2 * PAGES_PER_STEP * page, LANES)
    per_b3 = lambda bi, ji: (bi, 0, 0)
    keys3 = lambda bi, ji: (bi, 0, ji)
    return pl.pallas_call(
        _fox_decode_gathered_kernel,
        grid=(db, past // kb),
        in_specs=[
            pl.BlockSpec((1, ds, d), per_b3),
            pl.BlockSpec((1, ds, d), per_b3),
            pl.BlockSpec((1, page, d), per_b3),
            pl.BlockSpec((1, page, d), per_b3),
            pl.BlockSpec((1, nh, page), per_b3),
            pl.BlockSpec((1, d, kb), keys3),
            pl.BlockSpec((1, d, kb), keys3),
            pl.BlockSpec((1, nh, kb), keys3),
        ],
        out_specs=pl.BlockSpec((1, ds, d), per_b3),
        out_shape=jax.ShapeDtypeStruct((db, ds, d), F32),
        scratch_shapes=_decode_scratch(ds * nh, d),
        compiler_params=_cparams(("parallel", "arbitrary")),
        name="fox_decode_attention_gathered",
    )(q, gate, k_new, v_new, cn_bias, k_g, v_g, ck_g)


def _block_diag_ones(n, blk):
    i = jnp.arange(n)
    return (i[:, None] // blk == i[None, :] // blk).astype(BF16)


def _lower_tri(n, seq):
    i = jnp.arange(n)
    return ((i[:, None] >= i[None, :]) & (i[:, None] // seq == i[None, :] // seq)).astype(BF16)


def _prep_weights(w_ret_in, w_ret_out, w_kvf, b_f, g_k, w_fox_qg, g_q, w_fox_out, w_mlp_up, w_mlp_down):
    d = w_kvf.shape[0]
    nh = FOX_HEADS
    wf = jnp.zeros((d, LANES), BF16).at[:, :nh].set(w_kvf[:, 2 * d:].astype(BF16))
    bf = jnp.zeros((1, LANES), F32).at[0, :nh].set(b_f)
    per_layer = lambda a: [a[l].astype(BF16) for l in range(a.shape[0])]
    return dict(
        ret_in=per_layer(w_ret_in), ret_out=per_layer(w_ret_out),
        wk=w_kvf[:, :d].astype(BF16), wv=w_kvf[:, d:2 * d].T.astype(BF16), wf=wf, bf=bf,
        gk_t=jnp.tile(g_k, nh).reshape(1, d),
        wq=per_layer(w_fox_qg[:, :, :d]), wg=per_layer(w_fox_qg[:, :, d:]),
        gq_t=(jnp.tile(g_q, (1, nh)) * (FOX_DH ** -0.5 * LOG2E)).reshape(-1, 1, d),
        fox_out=per_layer(w_fox_out), up=per_layer(w_mlp_up), down=per_layer(w_mlp_down),
        bd=_block_diag_ones(MXU_DIM, FOX_DH),
    )


def _prompt_forward(x_prompt, meta, g_attn, g_mlp, w):
    b, seq, d = x_prompt.shape
    pad = RET_CHUNK - N_META
    x = jnp.concatenate([jnp.zeros((b, pad, d), F32),
                         jnp.broadcast_to(meta[None], (b, N_META, d)), x_prompt], axis=1)
    t = x.shape[1]
    pos = jnp.arange(t) - pad
    valid = pos >= 0
    x = x.reshape(b * t, d)
    n_ret = len(w["ret_in"])
    states = []
    for l in range(n_ret):
        p = norm_matmul(x, g_attn[l], w["ret_in"][l], BF16)
        s0 = jnp.zeros((1, b, RET_HEADS, 256, 512), F32)
        og, s_new = retention(p, s0, 0, RET_CHUNK, pos, valid, BF16)
        states.append(s_new)
        x = proj_mlp(x, og, w["ret_out"][l], g_mlp[l], w["up"][l], w["down"][l])

    tm = _tile(t, 512, LANES)
    kt, vt32, logf, _, ckm, kb, vtb = kv_proj(x, w["g_kv"], w["wk"], w["wv"], w["wf"], w["bf"], w["gk_t"],
                                              w["bd"], _lower_tri(tm, tm), b, pad)
    pre = RET_CHUNK
    nh = FOX_HEADS
    ck = ckm.reshape(b, t, nh // 2, 2).transpose(0, 2, 1, 3)
    kb3 = kb.reshape(b, t, d)
    vt = vtb.reshape(b, d // LANES, LANES, t)
    xr = x.reshape(b, t, d)[:, pre:].reshape(b * seq, d)
    for l in range(n_ret, g_attn.shape[0]):
        q, gate = fox_in(xr, g_attn[l], w["wq"][l - n_ret], w["wg"][l - n_ret], w["gq_t"][l - n_ret], w["bd"])
        a = fox_prompt_attention(q.reshape(b, seq, d), kb3, vt, gate.reshape(b, seq, d), ck, pre)
        xr = proj_mlp(xr, a.reshape(b * seq, d), w["fox_out"][l - n_ret], g_mlp[l], w["up"][l], w["down"][l])
    y = xr.reshape(b, seq, d)
    k4 = kt.reshape(b, nh, FOX_DH, t)[:, :, :, pad:].transpose(0, 3, 1, 2)
    v4 = vt32.reshape(b, nh, FOX_DH, t)[:, :, :, pad:].transpose(0, 3, 1, 2)
    return y, jnp.stack(states), k4, v4, logf.reshape(b, t, nh)[:, pad:]


def _sample_forward(x_sample, state_ret, cache_k, cache_v, cache_logf, page_table, g_attn, g_mlp, w):
    db, ds, d = x_sample.shape
    n_pool, page, nh, dh = cache_k.shape
    past = page_table.shape[1] * page
    pos = past + jnp.arange(ds)
    valid = jnp.ones((ds,), bool)
    x = x_sample.reshape(db * ds, d)
    n_ret = len(w["ret_in"])
    states = []
    for l in range(n_ret):
        p = norm_matmul(x, g_attn[l], w["ret_in"][l], F32)
        og, s_new = retention(p, state_ret, l, ds, pos, valid, F32)
        states.append(s_new)
        x = proj_mlp(x, og, w["ret_out"][l], g_mlp[l], w["up"][l], w["down"][l])

    n = db * ds
    kt, vt32, logf, cum, _, kb, vtb = kv_proj(x, w["g_kv"], w["wk"], w["wv"], w["wf"], w["bf"], w["gk_t"],
                                              w["bd"], _lower_tri(n, ds), 1, 0)
    k, v = kt[0].T, vt32[0].T
    zrows = jnp.zeros((db, page - ds, d), BF16)
    k_new = jnp.concatenate([kb.reshape(db, ds, d), zrows], axis=1)
    v_new = jnp.concatenate([vtb[0].T.reshape(db, ds, d), zrows], axis=1)
    cum_t = jnp.zeros((db, nh, page), F32).at[:, :, :ds].set(cum.reshape(db, ds, nh).transpose(0, 2, 1))
    tri_u = _lower_tri(page, page).T
    ck3 = cache_k.transpose(0, 2, 3, 1).reshape(n_pool, d, page)
    cv3 = cache_v.transpose(0, 2, 3, 1).reshape(n_pool, d, page)
    clf_t = cache_logf.transpose(0, 2, 1)
    for l in range(n_ret, g_attn.shape[0]):
        q, gate = fox_in(x, g_attn[l], w["wq"][l - n_ret], w["wg"][l - n_ret], w["gq_t"][l - n_ret], w["bd"])
        q3, gate3 = q.reshape(db, ds, d), gate.reshape(db, ds, d)
        if l == n_ret:
            a, k_g, v_g, ck_g, cn_bias = fox_decode_attention_paged(
                page_table, q3, gate3, k_new, v_new, cum_t, tri_u, ck3, cv3, clf_t)
        else:
            a = fox_decode_attention_gathered(q3, gate3, k_new, v_new, cn_bias, k_g, v_g, ck_g)
        x = proj_mlp(x, a.reshape(n, d), w["fox_out"][l - n_ret], g_mlp[l], w["up"][l], w["down"][l])
    return (x.reshape(db, ds, d), jnp.stack(states), k.reshape(db, ds, nh, dh), v.reshape(db, ds, nh, dh),
            logf.reshape(db, ds, nh))


def kernel(x_prompt, x_sample, state_ret, cache_k, cache_v, cache_logf, page_table, meta, g_attn, g_mlp,
           w_ret_in, w_ret_out, g_kv, w_kvf, b_f, g_k, w_fox_qg, g_q, w_fox_out, w_mlp_up, w_mlp_down):
    w = _prep_weights(w_ret_in, w_ret_out, w_kvf, b_f, g_k, w_fox_qg, g_q, w_fox_out, w_mlp_up, w_mlp_down)
    w["g_kv"] = g_kv
    y_p, s_p, k_p, v_p, lf_p = _prompt_forward(x_prompt, meta, g_attn, g_mlp, w)
    y_s, s_s, k_s, v_s, lf_s = _sample_forward(x_sample, state_ret, cache_k, cache_v, cache_logf, page_table,
                                               g_attn, g_mlp, w)
    return (y_p, y_s, s_p, s_s, k_p, v_p, lf_p, k_s, v_s, lf_s)
```

```python
import functools

import jax
import jax.numpy as jnp
from jax import lax
from jax.experimental import pallas as pl
from jax.experimental.pallas import tpu as pltpu

F32 = jnp.float32
BF16 = jnp.bfloat16

N_META = 16
RET_HEADS = 4
RET_CHUNK = 128
ROPE_BASE = 10000.0
FOX_HEADS = 16
FOX_DH = 64
EPS = 1e-6
MASK_VALUE = -1e30

LANES = 128
MXU_DIM = 256
VMEM_LIMIT = 56 * 1024 * 1024
PAGES_PER_STEP = 8
DENOM_ROWS = 16
ATTN_PAIRS_PER_STEP = 2
LOG2E = 1.4426950408889634


def _cparams(sem):
    return pltpu.CompilerParams(dimension_semantics=sem, vmem_limit_bytes=VMEM_LIMIT)


def _tile(n, target, mult=8):
    best = None
    for t in range(mult, min(n, target) + 1, mult):
        if n % t == 0:
            best = t
    assert best is not None, (n, target, mult)
    return best


def _rms(x, g_row):
    ms = jnp.mean(x * x, axis=-1, keepdims=True)
    return x * lax.rsqrt(ms + EPS) * g_row


def _split2(x):
    hi = x.astype(BF16)
    lo = (x - hi.astype(F32)).astype(BF16)
    return hi, lo


def _split3(x):
    hi = x.astype(BF16)
    r = x - hi.astype(F32)
    mid = r.astype(BF16)
    lo = (r - mid.astype(F32)).astype(BF16)
    return hi, mid, lo


def _head_rms(x, bd_ref, g_row):
    xx = x * x
    hi, lo = _split2(xx)
    bd = bd_ref[...]
    parts = []
    for c in range(x.shape[1] // MXU_DIM):
        sl = slice(c * MXU_DIM, (c + 1) * MXU_DIM)
        parts.append(jnp.dot(hi[:, sl], bd, preferred_element_type=F32)
                     + jnp.dot(lo[:, sl], bd, preferred_element_type=F32))
    ss = jnp.concatenate(parts, axis=-1)
    return x * lax.rsqrt(ss * (1.0 / FOX_DH) + EPS) * g_row


def _norm_matmul_kernel(x_ref, g_ref, w_ref, o_ref, xn_ref):
    @pl.when(pl.program_id(1) == 0)
    def _():
        xn_ref[...] = _rms(x_ref[...], g_ref[...]).astype(BF16)

    o_ref[...] = jnp.dot(xn_ref[...], w_ref[...], preferred_element_type=F32).astype(o_ref.dtype)


def norm_matmul(x, g, w_all, layer, out_dtype):
    n, d = x.shape
    nout = w_all.shape[2]
    tm = _tile(n, 1536)
    tn = _tile(nout, 1024, LANES)
    return pl.pallas_call(
        _norm_matmul_kernel,
        grid=(n // tm, nout // tn),
        in_specs=[
            pl.BlockSpec((tm, d), lambda i, j: (i, 0)),
            pl.BlockSpec((1, d), lambda i, j: (0, 0)),
            pl.BlockSpec((None, d, tn), lambda i, j: (layer, 0, j)),
        ],
        out_specs=pl.BlockSpec((tm, tn), lambda i, j: (i, j)),
        out_shape=jax.ShapeDtypeStruct((n, nout), out_dtype),
        scratch_shapes=[pltpu.VMEM((tm, d), BF16)],
        compiler_params=_cparams(("parallel", "arbitrary")),
        name="norm_matmul",
    )(x, g.reshape(1, d), w_all)


def _retention_kernel(q_ref, k_ref, v_ref, gt_ref, cq_ref, sq_ref, ck_ref, sk_ref,
                      dm_ref, qd_ref, kd_ref, gc_ref, s0_ref, og_ref, s_ref):
    @pl.when(pl.program_id(1) == 0)
    def _():
        s_ref[0] = s0_ref[0, 0]

    nh, dk, dv = s_ref.shape[1:]
    half = dk // 2
    cq, sq = cq_ref[...], sq_ref[...]
    ck, sk = ck_ref[...], sk_ref[...]
    for h in range(nh):
        q1 = q_ref[:, h * dk:h * dk + half].astype(F32)
        q2 = q_ref[:, h * dk + half:(h + 1) * dk].astype(F32)
        k1 = k_ref[:, h * dk:h * dk + half].astype(F32)
        k2 = k_ref[:, h * dk + half:(h + 1) * dk].astype(F32)
        qr = jnp.concatenate([q1 * cq - q2 * sq, q2 * cq + q1 * sq], axis=-1)
        kr = jnp.concatenate([k1 * ck - k2 * sk, k2 * ck + k1 * sk], axis=-1)
        qb = qr.astype(BF16)
        kb = kr.astype(BF16)
        vb = v_ref[:, h * dv:(h + 1) * dv].astype(BF16)
        s_old = s_ref[0, h]

        scores = lax.dot_general(qb, kb, (((1,), (1,)), ((), ())),
                                 preferred_element_type=F32) * dm_ref[h]
        intra = jnp.dot(scores.astype(BF16), vb, preferred_element_type=F32)
        cross = jnp.dot(qb, s_old.astype(BF16), preferred_element_type=F32) * qd_ref[h]
        o = intra + cross

        kd = (kr * kd_ref[h]).astype(BF16)
        s_ref[0, h] = gc_ref[h] * s_old + lax.dot_general(
            kd, vb, (((0,), (0,)), ((), ())), preferred_element_type=F32)

        on = o * lax.rsqrt(jnp.mean(o * o, axis=-1, keepdims=True) + EPS)
        gt = gt_ref[:, h * dv:(h + 1) * dv].astype(F32)
        og_ref[:, h * dv:(h + 1) * dv] = (gt * jax.nn.sigmoid(gt) * on).astype(og_ref.dtype)


def _retention_tables(chunk, pos, valid):
    dk = 256
    half = dk // 2
    lg = jnp.log1p(-jnp.exp2(-5.0 - jnp.arange(RET_HEADS, dtype=F32)))
    idx = jnp.arange(chunk, dtype=F32)
    diff = idx[:, None] - idx[None, :]
    dmat = jnp.where(diff >= 0, jnp.exp(lg[:, None, None] * jnp.maximum(diff, 0.0)), 0.0)
    qdec = jnp.exp(lg[:, None] * (idx[None, :] + 1.0))[:, :, None]
    kdec = jnp.exp(lg[:, None] * (chunk - 1.0 - idx[None, :]))[:, :, None]
    gc = jnp.exp(lg * chunk)[:, None, None]
    inv_freq = ROPE_BASE ** (-jnp.arange(half, dtype=F32) / half)
    ang = pos.astype(F32)[:, None] * inv_freq[None, :]
    cos, sin = jnp.cos(ang), jnp.sin(ang)
    kscale = (dk ** -0.5) * valid.astype(F32)[:, None]
    return dmat, qdec, kdec, gc, cos, sin, cos * kscale, sin * kscale


def retention(p, s0_all, layer, chunk, pos, valid, out_dtype):
    _, b, h, dk, dv = s0_all.shape
    n = p.shape[0]
    t = n // b
    nc = t // chunk
    dmat, qdec, kdec, gc, cq, sq, ck, sk = _retention_tables(chunk, pos, valid)
    row = lambda bi, ci: bi * nc + ci
    kblk = (h * dk) // (h * dk)
    vblk = (2 * h * dk) // (h * dv)
    gblk = (2 * h * dk + h * dv) // (h * dv)
    rope_spec = pl.BlockSpec((chunk, dk // 2), lambda bi, ci: (ci, 0))
    full3 = lambda bi, ci: (0, 0, 0)
    return pl.pallas_call(
        _retention_kernel,
        grid=(b, nc),
        in_specs=[
            pl.BlockSpec((chunk, h * dk), lambda bi, ci: (row(bi, ci), 0)),
            pl.BlockSpec((chunk, h * dk), lambda bi, ci: (row(bi, ci), kblk)),
            pl.BlockSpec((chunk, h * dv), lambda bi, ci: (row(bi, ci), vblk)),
            pl.BlockSpec((chunk, h * dv), lambda bi, ci: (row(bi, ci), gblk)),
            rope_spec, rope_spec, rope_spec, rope_spec,
            pl.BlockSpec((h, chunk, chunk), full3),
            pl.BlockSpec((h, chunk, 1), full3),
            pl.BlockSpec((h, chunk, 1), full3),
            pl.BlockSpec((h, 1, 1), full3),
            pl.BlockSpec((1, 1, h, dk, dv), lambda bi, ci: (layer, bi, 0, 0, 0)),
        ],
        out_specs=[
            pl.BlockSpec((chunk, h * dv), lambda bi, ci: (row(bi, ci), 0)),
            pl.BlockSpec((1, h, dk, dv), lambda bi, ci: (bi, 0, 0, 0)),
        ],
        out_shape=[
            jax.ShapeDtypeStruct((n, h * dv), out_dtype),
            jax.ShapeDtypeStruct((b, h, dk, dv), F32),
        ],
        compiler_params=_cparams(("parallel", "arbitrary")),
        name="retention",
    )(p, p, p, p, cq, sq, ck, sk, dmat, qdec, kdec, gc, s0_all)


def _proj_mlp_kernel(x_ref, a_ref, wo_ref, g_ref, wu_ref, wd_ref, o_ref, xn_ref):
    @pl.when(pl.program_id(1) == 0)
    def _():
        x1 = x_ref[...] + jnp.dot(a_ref[...].astype(BF16), wo_ref[...], preferred_element_type=F32)
        o_ref[...] = x1
        xn_ref[...] = _rms(x1, g_ref[...]).astype(BF16)

    hdn = jnp.dot(xn_ref[...], wu_ref[...], preferred_element_type=F32)
    hdn = jnp.square(jnp.maximum(hdn, 0.0)).astype(BF16)
    o_ref[...] += jnp.dot(hdn, wd_ref[...], preferred_element_type=F32)


def proj_mlp(x, a, wo_all, lo, g, wu_all, wd_all, lm):
    n, d = x.shape
    ka = a.shape[1]
    dff = wu_all.shape[2]
    tm = _tile(n, 1056, 16)
    tf = _tile(dff, 1024, LANES)
    return pl.pallas_call(
        _proj_mlp_kernel,
        grid=(n // tm, dff // tf),
        in_specs=[
            pl.BlockSpec((tm, d), lambda i, f: (i, 0)),
            pl.BlockSpec((tm, ka), lambda i, f: (i, 0)),
            pl.BlockSpec((None, ka, d), lambda i, f: (lo, 0, 0)),
            pl.BlockSpec((1, d), lambda i, f: (0, 0)),
            pl.BlockSpec((None, d, tf), lambda i, f: (lm, 0, f)),
            pl.BlockSpec((None, tf, d), lambda i, f: (lm, f, 0)),
        ],
        out_specs=pl.BlockSpec((tm, d), lambda i, f: (i, 0)),
        out_shape=jax.ShapeDtypeStruct((n, d), F32),
        scratch_shapes=[pltpu.VMEM((tm, d), BF16)],
        compiler_params=_cparams(("parallel", "arbitrary")),
        name="proj_mlp",
    )(x, a, wo_all, g.reshape(1, d), wu_all, wd_all)


def _kv_kernel(x_ref, g_ref, wk_ref, wv_ref, wf_ref, bf_ref, gk_ref, bd_ref, tri_ref,
               kt_ref, vt_ref, lf_ref, cum_ref, ckm_ref, kb_ref, vtb_ref, carry_ref, *, pad):
    t = pl.program_id(1)
    tm = x_ref.shape[0]

    @pl.when(t == 0)
    def _():
        carry_ref[...] = jnp.zeros_like(carry_ref)

    xn = _rms(x_ref[...], g_ref[...]).astype(BF16)
    kraw = jnp.dot(xn, wk_ref[...], preferred_element_type=F32)
    k = _head_rms(kraw, bd_ref, gk_ref[...])
    vt = lax.dot_general(wv_ref[...], xn, (((1,), (1,)), ((), ())), preferred_element_type=F32)
    kt_ref[0] = k.T
    vt_ref[0] = vt
    kb_ref[...] = k.astype(BF16)
    vtb_ref[0] = vt.astype(BF16)

    z = jnp.dot(xn, wf_ref[...], preferred_element_type=F32) + bf_ref[...]
    logf = jnp.minimum(z, 0.0) - jnp.log1p(jnp.exp(-jnp.abs(z)))
    rows = t * tm + lax.broadcasted_iota(jnp.int32, (tm, LANES), 0)
    valid = rows >= pad
    logf = jnp.where(valid, logf, 0.0)
    hi, mid, lo = _split3(logf)
    tri = tri_ref[...]
    cum = carry_ref[...] + (jnp.dot(tri, hi, preferred_element_type=F32)
                            + jnp.dot(tri, mid, preferred_element_type=F32)
                            + jnp.dot(tri, lo, preferred_element_type=F32))
    carry_ref[...] = cum[tm - 1:tm, :]
    nh = lf_ref.shape[1]
    lf_ref[...] = logf[:, :nh]
    cum_ref[...] = cum[:, :nh]
    ckm_ref[...] = jnp.where(valid, cum * LOG2E, -MASK_VALUE)[:, :nh]


def kv_proj(x, g, wk, wv, wf, bf, gk_t, bd, tri, nb, pad):
    n, d = x.shape
    tm = tri.shape[0]
    tb = n // nb
    nt = max(tb // tm, 1)
    nbg = n // (tm * nt)
    nh = FOX_HEADS
    row = lambda bi, ti: (bi * nt + ti, 0)
    col = lambda bi, ti: (bi, 0, ti)
    full = lambda bi, ti: (0, 0)
    return pl.pallas_call(
        functools.partial(_kv_kernel, pad=pad),
        grid=(nbg, nt),
        in_specs=[
            pl.BlockSpec((tm, d), row),
            pl.BlockSpec((1, d), full),
            pl.BlockSpec((d, d), full),
            pl.BlockSpec((d, d), full),
            pl.BlockSpec((d, LANES), full),
            pl.BlockSpec((1, LANES), full),
            pl.BlockSpec((1, d), full),
            pl.BlockSpec((MXU_DIM, MXU_DIM), full),
            pl.BlockSpec((tm, tm), full),
        ],
        out_specs=[
            pl.BlockSpec((1, d, tm), col),
            pl.BlockSpec((1, d, tm), col),
            pl.BlockSpec((tm, nh), row),
            pl.BlockSpec((tm, nh), row),
            pl.BlockSpec((tm, nh), row),
            pl.BlockSpec((tm, d), row),
            pl.BlockSpec((1, d, tm), col),
        ],
        out_shape=[
            jax.ShapeDtypeStruct((nbg, d, nt * tm), F32),
            jax.ShapeDtypeStruct((nbg, d, nt * tm), F32),
            jax.ShapeDtypeStruct((n, nh), F32),
            jax.ShapeDtypeStruct((n, nh), F32),
            jax.ShapeDtypeStruct((n, nh), F32),
            jax.ShapeDtypeStruct((n, d), BF16),
            jax.ShapeDtypeStruct((nbg, d, nt * tm), BF16),
        ],
        scratch_shapes=[pltpu.VMEM((1, LANES), F32)],
        compiler_params=_cparams(("parallel", "arbitrary")),
        name="kv_proj",
    )(x, g.reshape(1, d), wk, wv, wf, bf, gk_t, bd, tri)


def _fox_in_kernel(x_ref, g_ref, wq_ref, wg_ref, gq_ref, bd_ref, q_ref, gate_ref):
    xn = _rms(x_ref[...], g_ref[...]).astype(BF16)
    qraw = jnp.dot(xn, wq_ref[...], preferred_element_type=F32)
    q_ref[...] = _head_rms(qraw, bd_ref, gq_ref[...]).astype(BF16)
    gate_ref[...] = jnp.dot(xn, wg_ref[...], preferred_element_type=F32).astype(gate_ref.dtype)


def fox_in(x, g, wq, wg, gq_t, bd):
    n, d = x.shape
    tm = _tile(n, 1024, 16)
    row = lambda i: (i, 0)
    full = lambda i: (0, 0)
    return pl.pallas_call(
        _fox_in_kernel,
        grid=(n // tm,),
        in_specs=[
            pl.BlockSpec((tm, d), row),
            pl.BlockSpec((1, d), full),
            pl.BlockSpec((d, d), full),
            pl.BlockSpec((d, d), full),
            pl.BlockSpec((1, d), full),
            pl.BlockSpec((MXU_DIM, MXU_DIM), full),
        ],
        out_specs=[pl.BlockSpec((tm, d), row), pl.BlockSpec((tm, d), row)],
        out_shape=[jax.ShapeDtypeStruct((n, d), BF16), jax.ShapeDtypeStruct((n, d), BF16)],
        compiler_params=_cparams(("parallel",)),
        name="fox_in",
    )(x, g.reshape(1, d), wq, wg, gq_t, bd)


def _fox_prompt_kernel(q_ref, k_ref, vt_ref, g_ref, ck_ref, o_ref, m_ref, acc_ref, sa_ref, sb_ref, *, pre):
    i = pl.program_id(2)
    tq = q_ref.shape[1]
    tk = tq
    npair = vt_ref.shape[1]
    low = lax.broadcasted_iota(jnp.int32, (tq, LANES), 1) < FOX_DH
    qs = []
    for e in range(npair):
        q = q_ref[0, :, e * LANES:(e + 1) * LANES]
        zero = jnp.zeros_like(q)
        qs.append(jnp.concatenate([jnp.where(low, q, zero), jnp.where(low, zero, q)], axis=0))

    m_ref[...] = jnp.full_like(m_ref, MASK_VALUE)
    acc_ref[...] = jnp.zeros_like(acc_ref)

    def scores(e, j0, width):
        kj = k_ref[0, pl.ds(j0, width), e * LANES:(e + 1) * LANES]
        s = lax.dot_general(kj, qs[e], (((1,), (1,)), ((), ())), preferred_element_type=F32)
        ck = ck_ref[0, e, pl.ds(j0, width), :]
        return s[:, :tq] - ck[:, 0:1], s[:, tq:] - ck[:, 1:2]

    def start(b):
        return pl.multiple_of(pre + b * tk, LANES)

    def qk(b, s_ref):
        for e in range(npair):
            s0, s1 = scores(e, start(b), tk)
            s_ref[e, :, :tq] = s0
            s_ref[e, :, tq:] = s1

    def soft(e, s, vtj):
        m_old = m_ref[e]
        m_new = jnp.maximum(m_old, jnp.max(s, axis=0, keepdims=True))
        p = jnp.exp2(s - m_new).astype(BF16)
        alpha = jnp.exp2(m_old - m_new)
        vt_ones = jnp.concatenate([vtj, jnp.ones((DENOM_ROWS, vtj.shape[1]), BF16)], axis=0)
        acc_ref[e] = alpha * acc_ref[e] + jnp.dot(vt_ones, p, preferred_element_type=F32)
        m_ref[e] = m_new

    def soft_block(b, s_ref):
        for e in range(npair):
            soft(e, s_ref[e], vt_ref[0, e, :, pl.ds(start(b), tk)])

    odd = (i & 1) == 1

    @pl.when(odd)
    def _():
        qk(0, sb_ref)
        qk(1, sa_ref)
        soft_block(0, sb_ref)

    @pl.when(jnp.logical_not(odd))
    def _():
        qk(0, sa_ref)

    base = i & 1

    def body(jj, carry):
        b0 = base + 2 * jj
        qk(b0 + 1, sb_ref)
        soft_block(b0, sa_ref)
        qk(b0 + 2, sa_ref)
        soft_block(b0 + 1, sb_ref)
        return carry

    lax.fori_loop(0, i >> 1, body, 0)

    key = lax.broadcasted_iota(jnp.int32, (tk, tq), 0)
    qry = lax.broadcasted_iota(jnp.int32, (tk, tq), 1)
    ok = key <= qry
    top = lax.broadcasted_iota(jnp.int32, (LANES, tq), 0) < FOX_DH
    for e in range(npair):
        sd0 = jnp.where(ok, sa_ref[e, :, :tq], MASK_VALUE)
        sd1 = jnp.where(ok, sa_ref[e, :, tq:], MASK_VALUE)
        sp0, sp1 = scores(e, 0, pre)
        s_last = jnp.concatenate([jnp.concatenate([sp0, sd0], axis=0), jnp.concatenate([sp1, sd1], axis=0)], axis=1)
        vt_last = jnp.concatenate([vt_ref[0, e, :, 0:pre], vt_ref[0, e, :, pl.ds(start(i), tk)]], axis=1)
        soft(e, s_last, vt_last)
    for e in range(npair):
        ot = acc_ref[e, 0:LANES, :] / acc_ref[e, LANES:LANES + 1, :]
        o = jnp.where(top, ot[:, :tq], ot[:, tq:]).T
        sl = slice(e * LANES, (e + 1) * LANES)
        o_ref[0, :, sl] = (o * jax.nn.sigmoid(g_ref[0, :, sl].astype(F32))).astype(o_ref.dtype)


def fox_prompt_attention(q, kb, vt, gate, ck, pre):
    b, tq_all, d = q.shape
    tk_all = kb.shape[1]
    npair = ATTN_PAIRS_PER_STEP
    w = npair * LANES
    tq = _tile(tq_all, 512, LANES)
    return pl.pallas_call(
        functools.partial(_fox_prompt_kernel, pre=pre),
        grid=(b, d // w, tq_all // tq),
        in_specs=[
            pl.BlockSpec((1, tq, w), lambda bi, hi, qi: (bi, qi, hi)),
            pl.BlockSpec((1, tk_all, w), lambda bi, hi, qi: (bi, 0, hi)),
            pl.BlockSpec((1, npair, LANES, tk_all), lambda bi, hi, qi: (bi, hi, 0, 0)),
            pl.BlockSpec((1, tq, w), lambda bi, hi, qi: (bi, qi, hi)),
            pl.BlockSpec((1, npair, tk_all, 2), lambda bi, hi, qi: (bi, hi, 0, 0)),
        ],
        out_specs=pl.BlockSpec((1, tq, w), lambda bi, hi, qi: (bi, qi, hi)),
        out_shape=jax.ShapeDtypeStruct((b, tq_all, d), BF16),
        scratch_shapes=[
            pltpu.VMEM((npair, 1, 2 * tq), F32),
            pltpu.VMEM((npair, LANES + DENOM_ROWS, 2 * tq), F32),
            pltpu.VMEM((npair, tq, 2 * tq), F32),
            pltpu.VMEM((npair, tq, 2 * tq), F32),
        ],
        compiler_params=_cparams(("parallel", "parallel", "arbitrary")),
        name="fox_prompt_attention",
    )(q, kb, vt, gate, ck)


def _decode_head_mask(shape, row_axis, lane_axis):
    hrow = lax.broadcasted_iota(jnp.int32, shape, row_axis) & (FOX_HEADS - 1)
    hlane = lax.broadcasted_iota(jnp.int32, shape, lane_axis) >> 6
    return hrow == hlane


def _decode_init(q_ref, qbd_ref, m_ref, l_ref, acc_ref):
    ds, d = q_ref.shape[1:]
    q = q_ref[0].astype(F32)
    qrep = jnp.concatenate([jnp.broadcast_to(q[qi:qi + 1, :], (FOX_HEADS, d)) for qi in range(ds)], axis=0)
    qbd_ref[...] = jnp.where(_decode_head_mask(qrep.shape, 0, 1), qrep, 0.0).astype(BF16)
    m_ref[...] = jnp.full_like(m_ref, MASK_VALUE)
    l_ref[...] = jnp.zeros_like(l_ref)
    acc_ref[...] = jnp.zeros_like(acc_ref)


def _decode_update(m_ref, l_ref, acc_ref, s, v_bf, v_key_axis):
    m_old = m_ref[...]
    m_new = jnp.maximum(m_old, jnp.max(s, axis=-1, keepdims=True))
    p = jnp.exp2(s - m_new)
    alpha = jnp.exp2(m_old - m_new)
    l_ref[...] = alpha * l_ref[...] + jnp.sum(p, axis=-1, keepdims=True)
    acc_ref[...] = alpha * acc_ref[...] + lax.dot_general(
        p.astype(BF16), v_bf, (((1,), (v_key_axis,)), ((), ())), preferred_element_type=F32)
    m_ref[...] = m_new


def _decode_finish(qbd_ref, kn_ref, vn_ref, cn, g_ref, o_ref, m_ref, l_ref, acc_ref):
    ds, d = g_ref.shape[1:]
    nh = FOX_HEADS
    nr, page = ds * nh, kn_ref.shape[1]
    sn = lax.dot_general(qbd_ref[...], kn_ref[0], (((1,), (1,)), ((), ())),
                         preferred_element_type=F32) - jnp.concatenate([cn] * ds, axis=0)
    qi = lax.broadcasted_iota(jnp.int32, (nr, page), 0) >> 4
    kj = lax.broadcasted_iota(jnp.int32, (nr, page), 1)
    _decode_update(m_ref, l_ref, acc_ref, jnp.where(kj <= qi, sn, MASK_VALUE), vn_ref[0], 0)
    acc = acc_ref[...] / l_ref[...]
    acc = jnp.where(_decode_head_mask((nr, d), 0, 1), acc, 0.0)
    o = jnp.concatenate(
        [jnp.sum(acc[qi_ * nh:(qi_ + 1) * nh, :], axis=0, keepdims=True) for qi_ in range(ds)], axis=0)
    o_ref[0] = (o * jax.nn.sigmoid(g_ref[0].astype(F32))).astype(o_ref.dtype)


def _fox_decode_paged_kernel(pt_ref, q_ref, g_ref, kn_ref, vn_ref, cn_ref, tri_ref, *rest, pg):
    k_refs = rest[:pg]
    v_refs = rest[pg:2 * pg]
    lf_refs = rest[2 * pg:3 * pg]
    o_ref, kg_ref, vg_ref, ckg_ref, cng_ref = rest[3 * pg:3 * pg + 5]
    qbd_ref, m_ref, l_ref, acc_ref, carry_ref = rest[3 * pg + 5:]
    j = pl.program_id(1)
    nh = FOX_HEADS
    ds = q_ref.shape[1]
    page = k_refs[0].shape[2]

    @pl.when(j == 0)
    def _():
        _decode_init(q_ref, qbd_ref, m_ref, l_ref, acc_ref)
        carry_ref[...] = jnp.zeros_like(carry_ref)

    parts = []
    for pi in range(pg):
        kg_ref[0, :, pi * page:(pi + 1) * page] = k_refs[pi][0].astype(BF16)
        vg_ref[0, :, pi * page:(pi + 1) * page] = v_refs[pi][0].astype(BF16)
        parts.extend(_split3(lf_refs[pi][0]))
    w = jnp.dot(jnp.concatenate(parts, axis=0), tri_ref[...], preferred_element_type=F32)
    carry = carry_ref[...]
    cums = []
    for pi in range(pg):
        base = 3 * pi * nh
        wp = w[base:base + nh] + w[base + nh:base + 2 * nh] + w[base + 2 * nh:base + 3 * nh]
        cp = carry + wp
        cums.append(cp)
        carry = cp[:, page - 1:page]
    carry_ref[...] = carry
    ck = jnp.concatenate(cums, axis=-1) * LOG2E
    ckg_ref[0] = ck

    s = jnp.dot(qbd_ref[...], kg_ref[0], preferred_element_type=F32) - jnp.concatenate([ck] * ds, axis=0)
    _decode_update(m_ref, l_ref, acc_ref, s, vg_ref[0], 1)

    @pl.when(j == pl.num_programs(1) - 1)
    def _():
        cn = (carry_ref[...] + cn_ref[0]) * LOG2E
        cng_ref[0] = cn
        _decode_finish(qbd_ref, kn_ref, vn_ref, cn, g_ref, o_ref, m_ref, l_ref, acc_ref)


def _fox_decode_gathered_kernel(q_ref, g_ref, kn_ref, vn_ref, cng_ref, kg_ref, vg_ref, ckg_ref, o_ref,
                                qbd_ref, m_ref, l_ref, acc_ref):
    j = pl.program_id(1)
    ds = q_ref.shape[1]

    @pl.when(j == 0)
    def _():
        _decode_init(q_ref, qbd_ref, m_ref, l_ref, acc_ref)

    s = jnp.dot(qbd_ref[...], kg_ref[0], preferred_element_type=F32) - jnp.concatenate([ckg_ref[0]] * ds, axis=0)
    _decode_update(m_ref, l_ref, acc_ref, s, vg_ref[0], 1)

    @pl.when(j == pl.num_programs(1) - 1)
    def _():
        _decode_finish(qbd_ref, kn_ref, vn_ref, cng_ref[0], g_ref, o_ref, m_ref, l_ref, acc_ref)


def _decode_scratch(nr, d):
    return [pltpu.VMEM((nr, d), BF16), pltpu.VMEM((nr, 1), F32), pltpu.VMEM((nr, 1), F32), pltpu.VMEM((nr, d), F32)]


def fox_decode_attention_paged(page_table, q, gate, k_new, v_new, cum_new_t, tri_u, cache_k, cache_v, cache_lf_t):
    db, ds, d = q.shape
    npages = page_table.shape[1]
    page = cache_k.shape[2]
    pg = _tile(npages, PAGES_PER_STEP, 1)
    nh = FOX_HEADS
    past = npages * page

    def page_spec(shape, pi):
        return pl.BlockSpec(shape, lambda bi, ji, pt: (pt[bi, ji * pg + pi], 0, 0))

    per_b3 = lambda bi, ji, pt: (bi, 0, 0)
    keys3 = lambda bi, ji, pt: (bi, 0, ji)
    in_specs = [
        pl.BlockSpec((1, ds, d), per_b3),
        pl.BlockSpec((1, ds, d), per_b3),
        pl.BlockSpec((1, page, d), per_b3),
        pl.BlockSpec((1, page, d), per_b3),
        pl.BlockSpec((1, nh, page), per_b3),
        pl.BlockSpec((page, page), lambda bi, ji, pt: (0, 0)),
    ]
    in_specs += [page_spec((1, d, page), pi) for pi in range(pg)]
    in_specs += [page_spec((1, d, page), pi) for pi in range(pg)]
    in_specs += [page_spec((1, nh, page), pi) for pi in range(pg)]
    grid_spec = pltpu.PrefetchScalarGridSpec(
        num_scalar_prefetch=1,
        grid=(db, npages // pg),
        in_specs=in_specs,
        out_specs=[
            pl.BlockSpec((1, ds, d), per_b3),
            pl.BlockSpec((1, d, pg * page), keys3),
            pl.BlockSpec((1, d, pg * page), keys3),
            pl.BlockSpec((1, nh, pg * page), keys3),
            pl.BlockSpec((1, nh, page), per_b3),
        ],
        scratch_shapes=_decode_scratch(ds * nh, d) + [pltpu.VMEM((nh, 1), F32)],
    )
    return pl.pallas_call(
        functools.partial(_fox_decode_paged_kernel, pg=pg),
        grid_spec=grid_spec,
        out_shape=[
            jax.ShapeDtypeStruct((db, ds, d), F32),
            jax.ShapeDtypeStruct((db, d, past), BF16),
            jax.ShapeDtypeStruct((db, d, past), BF16),
            jax.ShapeDtypeStruct((db, nh, past), F32),
            jax.ShapeDtypeStruct((db, nh, page), F32),
        ],
        compiler_params=_cparams(("parallel", "arbitrary")),
        name="fox_decode_attention_paged",
    )(page_table, q, gate, k_new, v_new, cum_new_t, tri_u,
      *([cache_k] * pg), *([cache_v] * pg), *([cache_lf_t] * pg))


def fox_decode_attention_gathered(q, gate, k_new, v_new, cn_bias, k_g, v_g, ck_g):
    db, ds, d = q.shape
    past = k_g.shape[2]
    page = k_new.shape[1]
    nh = FOX_HEADS
    kb = _tile(past, 2 * PAGES_PER_STEP * page, LANES)
    per_b3 = lambda bi, ji: (bi, 0, 0)
    keys3 = lambda bi, ji: (bi, 0, ji)
    return pl.pallas_call(
        _fox_decode_gathered_kernel,
        grid=(db, past // kb),
        in_specs=[
            pl.BlockSpec((1, ds, d), per_b3),
            pl.BlockSpec((1, ds, d), per_b3),
            pl.BlockSpec((1, page, d), per_b3),
            pl.BlockSpec((1, page, d), per_b3),
            pl.BlockSpec((1, nh, page), per_b3),
            pl.BlockSpec((1, d, kb), keys3),
            pl.BlockSpec((1, d, kb), keys3),
            pl.BlockSpec((1, nh, kb), keys3),
        ],
        out_specs=pl.BlockSpec((1, ds, d), per_b3),
        out_shape=jax.ShapeDtypeStruct((db, ds, d), F32),
        scratch_shapes=_decode_scratch(ds * nh, d),
        compiler_params=_cparams(("parallel", "arbitrary")),
        name="fox_decode_attention_gathered",
    )(q, gate, k_new, v_new, cn_bias, k_g, v_g, ck_g)


def _block_diag_ones(n, blk):
    i = jnp.arange(n)
    return (i[:, None] // blk == i[None, :] // blk).astype(BF16)


def _lower_tri(n, seq):
    i = jnp.arange(n)
    return ((i[:, None] >= i[None, :]) & (i[:, None] // seq == i[None, :] // seq)).astype(BF16)


def _prep_weights(w_ret_in, w_ret_out, w_kvf, b_f, g_k, w_fox_qg, g_q, w_fox_out, w_mlp_up, w_mlp_down):
    d = w_kvf.shape[0]
    nh = FOX_HEADS
    wf = jnp.zeros((d, LANES), BF16).at[:, :nh].set(w_kvf[:, 2 * d:].astype(BF16))
    bf = jnp.zeros((1, LANES), F32).at[0, :nh].set(b_f)
    per_layer = lambda a: [a[l].astype(BF16) for l in range(a.shape[0])]
    return dict(
        ret_in=w_ret_in.astype(BF16), ret_out=w_ret_out.astype(BF16),
        wk=w_kvf[:, :d].astype(BF16), wv=w_kvf[:, d:2 * d].T.astype(BF16), wf=wf, bf=bf,
        gk_t=jnp.tile(g_k, nh).reshape(1, d),
        wq=per_layer(w_fox_qg[:, :, :d]), wg=per_layer(w_fox_qg[:, :, d:]),
        gq_t=(jnp.tile(g_q, (1, nh)) * (FOX_DH ** -0.5 * LOG2E)).reshape(-1, 1, d),
        fox_out=w_fox_out.astype(BF16), up=w_mlp_up.astype(BF16), down=w_mlp_down.astype(BF16),
        bd=_block_diag_ones(MXU_DIM, FOX_DH),
    )


def _prompt_forward(x_prompt, meta, g_attn, g_mlp, w):
    b, seq, d = x_prompt.shape
    pad = RET_CHUNK - N_META
    x = jnp.concatenate([jnp.zeros((b, pad, d), F32),
                         jnp.broadcast_to(meta[None], (b, N_META, d)), x_prompt], axis=1)
    t = x.shape[1]
    pos = jnp.arange(t) - pad
    valid = pos >= 0
    x = x.reshape(b * t, d)
    n_ret = w["ret_in"].shape[0]
    states = []
    for l in range(n_ret):
        p = norm_matmul(x, g_attn[l], w["ret_in"], l, BF16)
        s0 = jnp.zeros((1, b, RET_HEADS, 256, 512), F32)
        og, s_new = retention(p, s0, 0, RET_CHUNK, pos, valid, BF16)
        states.append(s_new)
        x = proj_mlp(x, og, w["ret_out"], l, g_mlp[l], w["up"], w["down"], l)

    tm = _tile(t, 512, LANES)
    kt, vt32, logf, _, ckm, kb, vtb = kv_proj(x, w["g_kv"], w["wk"], w["wv"], w["wf"], w["bf"], w["gk_t"],
                                              w["bd"], _lower_tri(tm, tm), b, pad)
    pre = RET_CHUNK
    nh = FOX_HEADS
    ck = ckm.reshape(b, t, nh // 2, 2).transpose(0, 2, 1, 3)
    kb3 = kb.reshape(b, t, d)
    vt = vtb.reshape(b, d // LANES, LANES, t)
    xr = x.reshape(b, t, d)[:, pre:].reshape(b * seq, d)
    for l in range(n_ret, g_attn.shape[0]):
        q, gate = fox_in(xr, g_attn[l], w["wq"][l - n_ret], w["wg"][l - n_ret], w["gq_t"][l - n_ret], w["bd"])
        a = fox_prompt_attention(q.reshape(b, seq, d), kb3, vt, gate.reshape(b, seq, d), ck, pre)
        xr = proj_mlp(xr, a.reshape(b * seq, d), w["fox_out"], l - n_ret, g_mlp[l], w["up"], w["down"], l)
    y = xr.reshape(b, seq, d)
    k4 = kt.reshape(b, nh, FOX_DH, t)[:, :, :, pad:].transpose(0, 3, 1, 2)
    v4 = vt32.reshape(b, nh, FOX_DH, t)[:, :, :, pad:].transpose(0, 3, 1, 2)
    return y, jnp.stack(states), k4, v4, logf.reshape(b, t, nh)[:, pad:]


def _sample_forward(x_sample, state_ret, cache_k, cache_v, cache_logf, page_table, g_attn, g_mlp, w):
    db, ds, d = x_sample.shape
    n_pool, page, nh, dh = cache_k.shape
    past = page_table.shape[1] * page
    pos = past + jnp.arange(ds)
    valid = jnp.ones((ds,), bool)
    x = x_sample.reshape(db * ds, d)
    n_ret = w["ret_in"].shape[0]
    states = []
    for l in range(n_ret):
        p = norm_matmul(x, g_attn[l], w["ret_in"], l, F32)
        og, s_new = retention(p, state_ret, l, ds, pos, valid, F32)
        states.append(s_new)
        x = proj_mlp(x, og, w["ret_out"], l, g_mlp[l], w["up"], w["down"], l)

    n = db * ds
    kt, vt32, logf, cum, _, kb, vtb = kv_proj(x, w["g_kv"], w["wk"], w["wv"], w["wf"], w["bf"], w["gk_t"],
                                              w["bd"], _lower_tri(n, ds), 1, 0)
    k, v = kt[0].T, vt32[0].T
    zrows = jnp.zeros((db, page - ds, d), BF16)
    k_new = jnp.concatenate([kb.reshape(db, ds, d), zrows], axis=1)
    v_new = jnp.concatenate([vtb[0].T.reshape(db, ds, d), zrows], axis=1)
    cum_t = jnp.zeros((db, nh, page), F32).at[:, :, :ds].set(cum.reshape(db, ds, nh).transpose(0, 2, 1))
    tri_u = _lower_tri(page, page).T
    ck3 = cache_k.transpose(0, 2, 3, 1).reshape(n_pool, d, page)
    cv3 = cache_v.transpose(0, 2, 3, 1).reshape(n_pool, d, page)
    clf_t = cache_logf.transpose(0, 2, 1)
    for l in range(n_ret, g_attn.shape[0]):
        q, gate = fox_in(x, g_attn[l], w["wq"][l - n_ret], w["wg"][l - n_ret], w["gq_t"][l - n_ret], w["bd"])
        q3, gate3 = q.reshape(db, ds, d), gate.reshape(db, ds, d)
        if l == n_ret:
            a, k_g, v_g, ck_g, cn_bias = fox_decode_attention_paged(
                page_table, q3, gate3, k_new, v_new, cum_t, tri_u, ck3, cv3, clf_t)
        else:
            a = fox_decode_attention_gathered(q3, gate3, k_new, v_new, cn_bias, k_g, v_g, ck_g)
        x = proj_mlp(x, a.reshape(n, d), w["fox_out"], l - n_ret, g_mlp[l], w["up"], w["down"], l)
    return (x.reshape(db, ds, d), jnp.stack(states), k.reshape(db, ds, nh, dh), v.reshape(db, ds, nh, dh),
            logf.reshape(db, ds, nh))


def kernel(x_prompt, x_sample, state_ret, cache_k, cache_v, cache_logf, page_table, meta, g_attn, g_mlp,
           w_ret_in, w_ret_out, g_kv, w_kvf, b_f, g_k, w_fox_qg, g_q, w_fox_out, w_mlp_up, w_mlp_down):
    w = _prep_weights(w_ret_in, w_ret_out, w_kvf, b_f, g_k, w_fox_qg, g_q, w_fox_out, w_mlp_up, w_mlp_down)
    w["g_kv"] = g_kv
    y_p, s_p, k_p, v_p, lf_p = _prompt_forward(x_prompt, meta, g_attn, g_mlp, w)
    y_s, s_s, k_s, v_s, lf_s = _sample_forward(x_sample, state_ret, cache_k, cache_v, cache_logf, page_table,
                                               g_attn, g_mlp, w)
    return (y_p, y_s, s_p, s_s, k_p, v_p, lf_p, k_s, v_s, lf_s)
```

```python
import functools

import jax
import jax.numpy as jnp
from jax import lax
from jax.experimental import pallas as pl
from jax.experimental.pallas import tpu as pltpu

F32 = jnp.float32
BF16 = jnp.bfloat16

N_META = 16
RET_HEADS = 4
RET_CHUNK = 128
ROPE_BASE = 10000.0
FOX_HEADS = 16
FOX_DH = 64
EPS = 1e-6
MASK_VALUE = -1e30

LANES = 128
MXU_DIM = 256
VMEM_LIMIT = 56 * 1024 * 1024
PAGES_PER_STEP = 8
DENOM_ROWS = 16
ATTN_PAIRS_PER_STEP = 2
LOG2E = 1.4426950408889634


def _cparams(sem):
    return pltpu.CompilerParams(dimension_semantics=sem, vmem_limit_bytes=VMEM_LIMIT)


def _tile(n, target, mult=8):
    best = None
    for t in range(mult, min(n, target) + 1, mult):
        if n % t == 0:
            best = t
    assert best is not None, (n, target, mult)
    return best


def _rms(x, g_row):
    ms = jnp.mean(x * x, axis=-1, keepdims=True)
    return x * lax.rsqrt(ms + EPS) * g_row


def _split2(x):
    hi = x.astype(BF16)
    lo = (x - hi.astype(F32)).astype(BF16)
    return hi, lo


def _split3(x):
    hi = x.astype(BF16)
    r = x - hi.astype(F32)
    mid = r.astype(BF16)
    lo = (r - mid.astype(F32)).astype(BF16)
    return hi, mid, lo


def _head_rms(x, bd_ref, g_row):
    xx = x * x
    hi, lo = _split2(xx)
    bd = bd_ref[...]
    parts = []
    for c in range(x.shape[1] // MXU_DIM):
        sl = slice(c * MXU_DIM, (c + 1) * MXU_DIM)
        parts.append(jnp.dot(hi[:, sl], bd, preferred_element_type=F32)
                     + jnp.dot(lo[:, sl], bd, preferred_element_type=F32))
    ss = jnp.concatenate(parts, axis=-1)
    return x * lax.rsqrt(ss * (1.0 / FOX_DH) + EPS) * g_row


def _norm_matmul_kernel(x_ref, g_ref, w_ref, o_ref, xn_ref):
    @pl.when(pl.program_id(1) == 0)
    def _():
        xn_ref[...] = _rms(x_ref[...], g_ref[...]).astype(BF16)

    o_ref[...] = jnp.dot(xn_ref[...], w_ref[...], preferred_element_type=F32).astype(o_ref.dtype)


def norm_matmul(x, g, w_all, layer, out_dtype):
    n, d = x.shape
    nout = w_all.shape[2]
    tm = _tile(n, 1536)
    tn = _tile(nout, 1024, LANES)
    return pl.pallas_call(
        _norm_matmul_kernel,
        grid=(n // tm, nout // tn),
        in_specs=[
            pl.BlockSpec((tm, d), lambda i, j: (i, 0)),
            pl.BlockSpec((1, d), lambda i, j: (0, 0)),
            pl.BlockSpec((None, d, tn), lambda i, j: (layer, 0, j)),
        ],
        out_specs=pl.BlockSpec((tm, tn), lambda i, j: (i, j)),
        out_shape=jax.ShapeDtypeStruct((n, nout), out_dtype),
        scratch_shapes=[pltpu.VMEM((tm, d), BF16)],
        compiler_params=_cparams(("parallel", "arbitrary")),
        name="norm_matmul",
    )(x, g.reshape(1, d), w_all)


def _retention_kernel(q_ref, k_ref, v_ref, gt_ref, cq_ref, sq_ref, ck_ref, sk_ref,
                      dm_ref, qd_ref, kd_ref, gc_ref, s0_ref, *rest):
    og_ref, s_ref = rest[-2:]
    @pl.when(pl.program_id(1) == 0)
    def _():
        s_ref[0] = s0_ref[0, 0]

    nh, dk, dv = s_ref.shape[1:]
    half = dk // 2
    cq, sq = cq_ref[...], sq_ref[...]
    ck, sk = ck_ref[...], sk_ref[...]
    for h in range(nh):
        q1 = q_ref[:, h * dk:h * dk + half].astype(F32)
        q2 = q_ref[:, h * dk + half:(h + 1) * dk].astype(F32)
        k1 = k_ref[:, h * dk:h * dk + half].astype(F32)
        k2 = k_ref[:, h * dk + half:(h + 1) * dk].astype(F32)
        qr = jnp.concatenate([q1 * cq - q2 * sq, q2 * cq + q1 * sq], axis=-1)
        kr = jnp.concatenate([k1 * ck - k2 * sk, k2 * ck + k1 * sk], axis=-1)
        qb = qr.astype(BF16)
        kb = kr.astype(BF16)
        vb = v_ref[:, h * dv:(h + 1) * dv].astype(BF16)
        s_old = s_ref[0, h]

        scores = lax.dot_general(qb, kb, (((1,), (1,)), ((), ())),
                                 preferred_element_type=F32) * dm_ref[h]
        intra = jnp.dot(scores.astype(BF16), vb, preferred_element_type=F32)
        cross = jnp.dot(qb, s_old.astype(BF16), preferred_element_type=F32) * qd_ref[h]
        o = intra + cross

        kd = (kr * kd_ref[h]).astype(BF16)
        s_ref[0, h] = gc_ref[h] * s_old + lax.dot_general(
            kd, vb, (((0,), (0,)), ((), ())), preferred_element_type=F32)

        on = o * lax.rsqrt(jnp.mean(o * o, axis=-1, keepdims=True) + EPS)
        gt = gt_ref[:, h * dv:(h + 1) * dv].astype(F32)
        og_ref[:, h * dv:(h + 1) * dv] = (gt * jax.nn.sigmoid(gt) * on).astype(og_ref.dtype)


def _retention_tables(chunk, pos, valid):
    dk = 256
    half = dk // 2
    lg = jnp.log1p(-jnp.exp2(-5.0 - jnp.arange(RET_HEADS, dtype=F32)))
    idx = jnp.arange(chunk, dtype=F32)
    diff = idx[:, None] - idx[None, :]
    dmat = jnp.where(diff >= 0, jnp.exp(lg[:, None, None] * jnp.maximum(diff, 0.0)), 0.0)
    qdec = jnp.exp(lg[:, None] * (idx[None, :] + 1.0))[:, :, None]
    kdec = jnp.exp(lg[:, None] * (chunk - 1.0 - idx[None, :]))[:, :, None]
    gc = jnp.exp(lg * chunk)[:, None, None]
    inv_freq = ROPE_BASE ** (-jnp.arange(half, dtype=F32) / half)
    ang = pos.astype(F32)[:, None] * inv_freq[None, :]
    cos, sin = jnp.cos(ang), jnp.sin(ang)
    kscale = (dk ** -0.5) * valid.astype(F32)[:, None]
    return dmat, qdec, kdec, gc, cos, sin, cos * kscale, sin * kscale


def retention(p, s0_all, s0_layer, states, layer, n_layers, chunk, pos, valid, out_dtype):
    _, b, h, dk, dv = s0_all.shape
    n = p.shape[0]
    t = n // b
    nc = t // chunk
    dmat, qdec, kdec, gc, cq, sq, ck, sk = _retention_tables(chunk, pos, valid)
    row = lambda bi, ci: bi * nc + ci
    kblk = (h * dk) // (h * dk)
    vblk = (2 * h * dk) // (h * dv)
    gblk = (2 * h * dk + h * dv) // (h * dv)
    rope_spec = pl.BlockSpec((chunk, dk // 2), lambda bi, ci: (ci, 0))
    full3 = lambda bi, ci: (0, 0, 0)
    carried = [] if states is None else [states]
    og, new_states = pl.pallas_call(
        _retention_kernel,
        grid=(b, nc),
        in_specs=[
            pl.BlockSpec((chunk, h * dk), lambda bi, ci: (row(bi, ci), 0)),
            pl.BlockSpec((chunk, h * dk), lambda bi, ci: (row(bi, ci), kblk)),
            pl.BlockSpec((chunk, h * dv), lambda bi, ci: (row(bi, ci), vblk)),
            pl.BlockSpec((chunk, h * dv), lambda bi, ci: (row(bi, ci), gblk)),
            rope_spec, rope_spec, rope_spec, rope_spec,
            pl.BlockSpec((h, chunk, chunk), full3),
            pl.BlockSpec((h, chunk, 1), full3),
            pl.BlockSpec((h, chunk, 1), full3),
            pl.BlockSpec((h, 1, 1), full3),
            pl.BlockSpec((1, 1, h, dk, dv), lambda bi, ci: (s0_layer, bi, 0, 0, 0)),
        ] + [pl.BlockSpec(memory_space=pl.ANY)] * len(carried),
        out_specs=[
            pl.BlockSpec((chunk, h * dv), lambda bi, ci: (row(bi, ci), 0)),
            pl.BlockSpec((None, 1, h, dk, dv), lambda bi, ci: (layer, bi, 0, 0, 0)),
        ],
        out_shape=[
            jax.ShapeDtypeStruct((n, h * dv), out_dtype),
            jax.ShapeDtypeStruct((n_layers, b, h, dk, dv), F32),
        ],
        input_output_aliases={13: 1} if carried else {},
        compiler_params=_cparams(("parallel", "arbitrary")),
        name="retention",
    )(p, p, p, p, cq, sq, ck, sk, dmat, qdec, kdec, gc, s0_all, *carried)
    return og, new_states


def _proj_mlp_kernel(x_ref, a_ref, wo_ref, g_ref, wu_ref, wd_ref, o_ref, xn_ref):
    @pl.when(pl.program_id(1) == 0)
    def _():
        x1 = x_ref[...] + jnp.dot(a_ref[...].astype(BF16), wo_ref[...], preferred_element_type=F32)
        o_ref[...] = x1
        xn_ref[...] = _rms(x1, g_ref[...]).astype(BF16)

    hdn = jnp.dot(xn_ref[...], wu_ref[...], preferred_element_type=F32)
    hdn = jnp.square(jnp.maximum(hdn, 0.0)).astype(BF16)
    o_ref[...] += jnp.dot(hdn, wd_ref[...], preferred_element_type=F32)


def proj_mlp(x, a, wo_all, lo, g, wu_all, wd_all, lm):
    n, d = x.shape
    ka = a.shape[1]
    dff = wu_all.shape[2]
    tm = _tile(n, 1056, 16)
    tf = _tile(dff, 1024, LANES)
    return pl.pallas_call(
        _proj_mlp_kernel,
        grid=(n // tm, dff // tf),
        in_specs=[
            pl.BlockSpec((tm, d), lambda i, f: (i, 0)),
            pl.BlockSpec((tm, ka), lambda i, f: (i, 0)),
            pl.BlockSpec((None, ka, d), lambda i, f: (lo, 0, 0)),
            pl.BlockSpec((1, d), lambda i, f: (0, 0)),
            pl.BlockSpec((None, d, tf), lambda i, f: (lm, 0, f)),
            pl.BlockSpec((None, tf, d), lambda i, f: (lm, f, 0)),
        ],
        out_specs=pl.BlockSpec((tm, d), lambda i, f: (i, 0)),
        out_shape=jax.ShapeDtypeStruct((n, d), F32),
        scratch_shapes=[pltpu.VMEM((tm, d), BF16)],
        compiler_params=_cparams(("parallel", "arbitrary")),
        name="proj_mlp",
    )(x, a, wo_all, g.reshape(1, d), wu_all, wd_all)


def _kv_kernel(x_ref, g_ref, wk_ref, wv_ref, wf_ref, bf_ref, gk_ref, bd_ref, tri_ref,
               kt_ref, vt_ref, lf_ref, cum_ref, ckm_ref, kb_ref, vtb_ref, carry_ref, *, pad):
    t = pl.program_id(1)
    tm = x_ref.shape[0]

    @pl.when(t == 0)
    def _():
        carry_ref[...] = jnp.zeros_like(carry_ref)

    xn = _rms(x_ref[...], g_ref[...]).astype(BF16)
    kraw = jnp.dot(xn, wk_ref[...], preferred_element_type=F32)
    k = _head_rms(kraw, bd_ref, gk_ref[...])
    vt = lax.dot_general(wv_ref[...], xn, (((1,), (1,)), ((), ())), preferred_element_type=F32)
    kt_ref[0] = k.T
    vt_ref[0] = vt
    kb_ref[...] = k.astype(BF16)
    vtb_ref[0] = vt.astype(BF16)

    z = jnp.dot(xn, wf_ref[...], preferred_element_type=F32) + bf_ref[...]
    logf = jnp.minimum(z, 0.0) - jnp.log1p(jnp.exp(-jnp.abs(z)))
    rows = t * tm + lax.broadcasted_iota(jnp.int32, (tm, LANES), 0)
    valid = rows >= pad
    logf = jnp.where(valid, logf, 0.0)
    hi, mid, lo = _split3(logf)
    tri = tri_ref[...]
    cum = carry_ref[...] + (jnp.dot(tri, hi, preferred_element_type=F32)
                            + jnp.dot(tri, mid, preferred_element_type=F32)
                            + jnp.dot(tri, lo, preferred_element_type=F32))
    carry_ref[...] = cum[tm - 1:tm, :]
    nh = lf_ref.shape[1]
    lf_ref[...] = logf[:, :nh]
    cum_ref[...] = cum[:, :nh]
    ckm_ref[...] = jnp.where(valid, cum * LOG2E, -MASK_VALUE)[:, :nh]


def kv_proj(x, g, wk, wv, wf, bf, gk_t, bd, tri, nb, pad):
    n, d = x.shape
    tm = tri.shape[0]
    tb = n // nb
    nt = max(tb // tm, 1)
    nbg = n // (tm * nt)
    nh = FOX_HEADS
    row = lambda bi, ti: (bi * nt + ti, 0)
    col = lambda bi, ti: (bi, 0, ti)
    full = lambda bi, ti: (0, 0)
    return pl.pallas_call(
        functools.partial(_kv_kernel, pad=pad),
        grid=(nbg, nt),
        in_specs=[
            pl.BlockSpec((tm, d), row),
            pl.BlockSpec((1, d), full),
            pl.BlockSpec((d, d), full),
            pl.BlockSpec((d, d), full),
            pl.BlockSpec((d, LANES), full),
            pl.BlockSpec((1, LANES), full),
            pl.BlockSpec((1, d), full),
            pl.BlockSpec((MXU_DIM, MXU_DIM), full),
            pl.BlockSpec((tm, tm), full),
        ],
        out_specs=[
            pl.BlockSpec((1, d, tm), col),
            pl.BlockSpec((1, d, tm), col),
            pl.BlockSpec((tm, nh), row),
            pl.BlockSpec((tm, nh), row),
            pl.BlockSpec((tm, nh), row),
            pl.BlockSpec((tm, d), row),
            pl.BlockSpec((1, d, tm), col),
        ],
        out_shape=[
            jax.ShapeDtypeStruct((nbg, d, nt * tm), F32),
            jax.ShapeDtypeStruct((nbg, d, nt * tm), F32),
            jax.ShapeDtypeStruct((n, nh), F32),
            jax.ShapeDtypeStruct((n, nh), F32),
            jax.ShapeDtypeStruct((n, nh), F32),
            jax.ShapeDtypeStruct((n, d), BF16),
            jax.ShapeDtypeStruct((nbg, d, nt * tm), BF16),
        ],
        scratch_shapes=[pltpu.VMEM((1, LANES), F32)],
        compiler_params=_cparams(("parallel", "arbitrary")),
        name="kv_proj",
    )(x, g.reshape(1, d), wk, wv, wf, bf, gk_t, bd, tri)


def _fox_in_kernel(x_ref, g_ref, wq_ref, wg_ref, gq_ref, bd_ref, q_ref, gate_ref):
    xn = _rms(x_ref[...], g_ref[...]).astype(BF16)
    qraw = jnp.dot(xn, wq_ref[...], preferred_element_type=F32)
    q_ref[...] = _head_rms(qraw, bd_ref, gq_ref[...]).astype(BF16)
    gate_ref[...] = jnp.dot(xn, wg_ref[...], preferred_element_type=F32).astype(gate_ref.dtype)


def fox_in(x, g, wq, wg, gq_t, bd):
    n, d = x.shape
    tm = _tile(n, 1024, 16)
    row = lambda i: (i, 0)
    full = lambda i: (0, 0)
    return pl.pallas_call(
        _fox_in_kernel,
        grid=(n // tm,),
        in_specs=[
            pl.BlockSpec((tm, d), row),
            pl.BlockSpec((1, d), full),
            pl.BlockSpec((d, d), full),
            pl.BlockSpec((d, d), full),
            pl.BlockSpec((1, d), full),
            pl.BlockSpec((MXU_DIM, MXU_DIM), full),
        ],
        out_specs=[pl.BlockSpec((tm, d), row), pl.BlockSpec((tm, d), row)],
        out_shape=[jax.ShapeDtypeStruct((n, d), BF16), jax.ShapeDtypeStruct((n, d), BF16)],
        compiler_params=_cparams(("parallel",)),
        name="fox_in",
    )(x, g.reshape(1, d), wq, wg, gq_t, bd)


def _fox_prompt_kernel(q_ref, k_ref, vt_ref, g_ref, ck_ref, o_ref, m_ref, acc_ref, sa_ref, sb_ref, *, pre):
    i = pl.program_id(2)
    tq = q_ref.shape[1]
    tk = tq
    npair = vt_ref.shape[1]
    low = lax.broadcasted_iota(jnp.int32, (tq, LANES), 1) < FOX_DH
    qs = []
    for e in range(npair):
        q = q_ref[0, :, e * LANES:(e + 1) * LANES]
        zero = jnp.zeros_like(q)
        qs.append(jnp.concatenate([jnp.where(low, q, zero), jnp.where(low, zero, q)], axis=0))

    m_ref[...] = jnp.full_like(m_ref, MASK_VALUE)
    acc_ref[...] = jnp.zeros_like(acc_ref)

    def scores(e, j0, width):
        kj = k_ref[0, pl.ds(j0, width), e * LANES:(e + 1) * LANES]
        s = lax.dot_general(kj, qs[e], (((1,), (1,)), ((), ())), preferred_element_type=F32)
        ck = ck_ref[0, e, pl.ds(j0, width), :]
        return s[:, :tq] - ck[:, 0:1], s[:, tq:] - ck[:, 1:2]

    def start(b):
        return pl.multiple_of(pre + b * tk, LANES)

    def qk(b, s_ref):
        for e in range(npair):
            s0, s1 = scores(e, start(b), tk)
            s_ref[e, :, :tq] = s0
            s_ref[e, :, tq:] = s1

    def soft(e, s, vtj):
        m_old = m_ref[e]
        m_new = jnp.maximum(m_old, jnp.max(s, axis=0, keepdims=True))
        p = jnp.exp2(s - m_new).astype(BF16)
        alpha = jnp.exp2(m_old - m_new)
        vt_ones = jnp.concatenate([vtj, jnp.ones((DENOM_ROWS, vtj.shape[1]), BF16)], axis=0)
        acc_ref[e] = alpha * acc_ref[e] + jnp.dot(vt_ones, p, preferred_element_type=F32)
        m_ref[e] = m_new

    def soft_block(b, s_ref):
        for e in range(npair):
            soft(e, s_ref[e], vt_ref[0, e, :, pl.ds(start(b), tk)])

    odd = (i & 1) == 1

    @pl.when(odd)
    def _():
        qk(0, sb_ref)
        qk(1, sa_ref)
        soft_block(0, sb_ref)

    @pl.when(jnp.logical_not(odd))
    def _():
        qk(0, sa_ref)

    base = i & 1

    def body(jj, carry):
        b0 = base + 2 * jj
        qk(b0 + 1, sb_ref)
        soft_block(b0, sa_ref)
        qk(b0 + 2, sa_ref)
        soft_block(b0 + 1, sb_ref)
        return carry

    lax.fori_loop(0, i >> 1, body, 0)

    key = lax.broadcasted_iota(jnp.int32, (tk, tq), 0)
    qry = lax.broadcasted_iota(jnp.int32, (tk, tq), 1)
    ok = key <= qry
    top = lax.broadcasted_iota(jnp.int32, (LANES, tq), 0) < FOX_DH
    for e in range(npair):
        sd0 = jnp.where(ok, sa_ref[e, :, :tq], MASK_VALUE)
        sd1 = jnp.where(ok, sa_ref[e, :, tq:], MASK_VALUE)
        sp0, sp1 = scores(e, 0, pre)
        s_last = jnp.concatenate([jnp.concatenate([sp0, sd0], axis=0), jnp.concatenate([sp1, sd1], axis=0)], axis=1)
        vt_last = jnp.concatenate([vt_ref[0, e, :, 0:pre], vt_ref[0, e, :, pl.ds(start(i), tk)]], axis=1)
        soft(e, s_last, vt_last)
    for e in range(npair):
        ot = acc_ref[e, 0:LANES, :] / acc_ref[e, LANES:LANES + 1, :]
        o = jnp.where(top, ot[:, :tq], ot[:, tq:]).T
        sl = slice(e * LANES, (e + 1) * LANES)
        o_ref[0, :, sl] = (o * jax.nn.sigmoid(g_ref[0, :, sl].astype(F32))).astype(o_ref.dtype)


def fox_prompt_attention(q, kb, vt, gate, ck, pre):
    b, tq_all, d = q.shape
    tk_all = kb.shape[1]
    npair = ATTN_PAIRS_PER_STEP
    w = npair * LANES
    tq = _tile(tq_all, 512, LANES)
    return pl.pallas_call(
        functools.partial(_fox_prompt_kernel, pre=pre),
        grid=(b, d // w, tq_all // tq),
        in_specs=[
            pl.BlockSpec((1, tq, w), lambda bi, hi, qi: (bi, qi, hi)),
            pl.BlockSpec((1, tk_all, w), lambda bi, hi, qi: (bi, 0, hi)),
            pl.BlockSpec((1, npair, LANES, tk_all), lambda bi, hi, qi: (bi, hi, 0, 0)),
            pl.BlockSpec((1, tq, w), lambda bi, hi, qi: (bi, qi, hi)),
            pl.BlockSpec((1, npair, tk_all, 2), lambda bi, hi, qi: (bi, hi, 0, 0)),
        ],
        out_specs=pl.BlockSpec((1, tq, w), lambda bi, hi, qi: (bi, qi, hi)),
        out_shape=jax.ShapeDtypeStruct((b, tq_all, d), BF16),
        scratch_shapes=[
            pltpu.VMEM((npair, 1, 2 * tq), F32),
            pltpu.VMEM((npair, LANES + DENOM_ROWS, 2 * tq), F32),
            pltpu.VMEM((npair, tq, 2 * tq), F32),
            pltpu.VMEM((npair, tq, 2 * tq), F32),
        ],
        compiler_params=_cparams(("parallel", "parallel", "arbitrary")),
        name="fox_prompt_attention",
    )(q, kb, vt, gate, ck)


def _decode_head_mask(shape, row_axis, lane_axis):
    hrow = lax.broadcasted_iota(jnp.int32, shape, row_axis) & (FOX_HEADS - 1)
    hlane = lax.broadcasted_iota(jnp.int32, shape, lane_axis) >> 6
    return hrow == hlane


def _decode_init(q_ref, qbd_ref, m_ref, l_ref, acc_ref):
    ds, d = q_ref.shape[1:]
    q = q_ref[0].astype(F32)
    qrep = jnp.concatenate([jnp.broadcast_to(q[qi:qi + 1, :], (FOX_HEADS, d)) for qi in range(ds)], axis=0)
    qbd_ref[...] = jnp.where(_decode_head_mask(qrep.shape, 0, 1), qrep, 0.0).astype(BF16)
    m_ref[...] = jnp.full_like(m_ref, MASK_VALUE)
    l_ref[...] = jnp.zeros_like(l_ref)
    acc_ref[...] = jnp.zeros_like(acc_ref)


def _decode_update(m_ref, l_ref, acc_ref, s, v_bf, v_key_axis):
    m_old = m_ref[...]
    m_new = jnp.maximum(m_old, jnp.max(s, axis=-1, keepdims=True))
    p = jnp.exp2(s - m_new)
    alpha = jnp.exp2(m_old - m_new)
    l_ref[...] = alpha * l_ref[...] + jnp.sum(p, axis=-1, keepdims=True)
    acc_ref[...] = alpha * acc_ref[...] + lax.dot_general(
        p.astype(BF16), v_bf, (((1,), (v_key_axis,)), ((), ())), preferred_element_type=F32)
    m_ref[...] = m_new


def _decode_finish(qbd_ref, kn_ref, vn_ref, cn, g_ref, o_ref, m_ref, l_ref, acc_ref):
    ds, d = g_ref.shape[1:]
    nh = FOX_HEADS
    nr, page = ds * nh, kn_ref.shape[1]
    sn = lax.dot_general(qbd_ref[...], kn_ref[0], (((1,), (1,)), ((), ())),
                         preferred_element_type=F32) - jnp.concatenate([cn] * ds, axis=0)
    qi = lax.broadcasted_iota(jnp.int32, (nr, page), 0) >> 4
    kj = lax.broadcasted_iota(jnp.int32, (nr, page), 1)
    _decode_update(m_ref, l_ref, acc_ref, jnp.where(kj <= qi, sn, MASK_VALUE), vn_ref[0], 0)
    acc = acc_ref[...] / l_ref[...]
    acc = jnp.where(_decode_head_mask((nr, d), 0, 1), acc, 0.0)
    o = jnp.concatenate(
        [jnp.sum(acc[qi_ * nh:(qi_ + 1) * nh, :], axis=0, keepdims=True) for qi_ in range(ds)], axis=0)
    o_ref[0] = (o * jax.nn.sigmoid(g_ref[0].astype(F32))).astype(o_ref.dtype)


def _fox_decode_paged_kernel(pt_ref, q_ref, g_ref, kn_ref, vn_ref, cn_ref, tri_ref, *rest, pg):
    k_refs = rest[:pg]
    v_refs = rest[pg:2 * pg]
    lf_refs = rest[2 * pg:3 * pg]
    o_ref, kg_ref, vg_ref, ckg_ref, cng_ref = rest[3 * pg:3 * pg + 5]
    qbd_ref, m_ref, l_ref, acc_ref, carry_ref = rest[3 * pg + 5:]
    j = pl.program_id(1)
    nh = FOX_HEADS
    ds = q_ref.shape[1]
    page = k_refs[0].shape[2]

    @pl.when(j == 0)
    def _():
        _decode_init(q_ref, qbd_ref, m_ref, l_ref, acc_ref)
        carry_ref[...] = jnp.zeros_like(carry_ref)

    parts = []
    for pi in range(pg):
        kg_ref[0, :, pi * page:(pi + 1) * page] = k_refs[pi][0].astype(BF16)
        vg_ref[0, :, pi * page:(pi + 1) * page] = v_refs[pi][0].astype(BF16)
        parts.extend(_split3(lf_refs[pi][0]))
    w = jnp.dot(jnp.concatenate(parts, axis=0), tri_ref[...], preferred_element_type=F32)
    carry = carry_ref[...]
    cums = []
    for pi in range(pg):
        base = 3 * pi * nh
        wp = w[base:base + nh] + w[base + nh:base + 2 * nh] + w[base + 2 * nh:base + 3 * nh]
        cp = carry + wp
        cums.append(cp)
        carry = cp[:, page - 1:page]
    carry_ref[...] = carry
    ck = jnp.concatenate(cums, axis=-1) * LOG2E
    ckg_ref[0] = ck

    s = jnp.dot(qbd_ref[...], kg_ref[0], preferred_element_type=F32) - jnp.concatenate([ck] * ds, axis=0)
    _decode_update(m_ref, l_ref, acc_ref, s, vg_ref[0], 1)

    @pl.when(j == pl.num_programs(1) - 1)
    def _():
        cn = (carry_ref[...] + cn_ref[0]) * LOG2E
        cng_ref[0] = cn
        _decode_finish(qbd_ref, kn_ref, vn_ref, cn, g_ref, o_ref, m_ref, l_ref, acc_ref)


def _fox_decode_gathered_kernel(q_ref, g_ref, kn_ref, vn_ref, cng_ref, kg_ref, vg_ref, ckg_ref, o_ref,
                                qbd_ref, m_ref, l_ref, acc_ref):
    j = pl.program_id(1)
    ds = q_ref.shape[1]

    @pl.when(j == 0)
    def _():
        _decode_init(q_ref, qbd_ref, m_ref, l_ref, acc_ref)

    s = jnp.dot(qbd_ref[...], kg_ref[0], preferred_element_type=F32) - jnp.concatenate([ckg_ref[0]] * ds, axis=0)
    _decode_update(m_ref, l_ref, acc_ref, s, vg_ref[0], 1)

    @pl.when(j == pl.num_programs(1) - 1)
    def _():
        _decode_finish(qbd_ref, kn_ref, vn_ref, cng_ref[0], g_ref, o_ref, m_ref, l_ref, acc_ref)


def _decode_scratch(nr, d):
    return [pltpu.VMEM((nr, d), BF16), pltpu.VMEM((nr, 1), F32), pltpu.VMEM((nr, 1), F32), pltpu.VMEM((nr, d), F32)]


def fox_decode_attention_paged(page_table, q, gate, k_new, v_new, cum_new_t, tri_u, cache_k, cache_v, cache_lf_t):
    db, ds, d = q.shape
    npages = page_table.shape[1]
    page = cache_k.shape[2]
    pg = _tile(npages, PAGES_PER_STEP, 1)
    nh = FOX_HEADS
    past = npages * page

    def page_spec(shape, pi):
        return pl.BlockSpec(shape, lambda bi, ji, pt: (pt[bi, ji * pg + pi], 0, 0))

    per_b3 = lambda bi, ji, pt: (bi, 0, 0)
    keys3 = lambda bi, ji, pt: (bi, 0, ji)
    in_specs = [
        pl.BlockSpec((1, ds, d), per_b3),
        pl.BlockSpec((1, ds, d), per_b3),
        pl.BlockSpec((1, page, d), per_b3),
        pl.BlockSpec((1, page, d), per_b3),
        pl.BlockSpec((1, nh, page), per_b3),
        pl.BlockSpec((page, page), lambda bi, ji, pt: (0, 0)),
    ]
    in_specs += [page_spec((1, d, page), pi) for pi in range(pg)]
    in_specs += [page_spec((1, d, page), pi) for pi in range(pg)]
    in_specs += [page_spec((1, nh, page), pi) for pi in range(pg)]
    grid_spec = pltpu.PrefetchScalarGridSpec(
        num_scalar_prefetch=1,
        grid=(db, npages // pg),
        in_specs=in_specs,
        out_specs=[
            pl.BlockSpec((1, ds, d), per_b3),
            pl.BlockSpec((1, d, pg * page), keys3),
            pl.BlockSpec((1, d, pg * page), keys3),
            pl.BlockSpec((1, nh, pg * page), keys3),
            pl.BlockSpec((1, nh, page), per_b3),
        ],
        scratch_shapes=_decode_scratch(ds * nh, d) + [pltpu.VMEM((nh, 1), F32)],
    )
    return pl.pallas_call(
        functools.partial(_fox_decode_paged_kernel, pg=pg),
        grid_spec=grid_spec,
        out_shape=[
            jax.ShapeDtypeStruct((db, ds, d), F32),
            jax.ShapeDtypeStruct((db, d, past), BF16),
            jax.ShapeDtypeStruct((db, d, past), BF16),
            jax.ShapeDtypeStruct((db, nh, past), F32),
            jax.ShapeDtypeStruct((db, nh, page), F32),
        ],
        compiler_params=_cparams(("parallel", "arbitrary")),
        name="fox_decode_attention_paged",
    )(page_table, q, gate, k_new, v_new, cum_new_t, tri_u,
      *([cache_k] * pg), *([cache_v] * pg), *([cache_lf_t] * pg))


def fox_decode_attention_gathered(q, gate, k_new, v_new, cn_bias, k_g, v_g, ck_g):
    db, ds, d = q.shape
    past = k_g.shape[2]
    page = k_new.shape[1]
    nh = FOX_HEADS
    kb = _tile(past, 2 * PAGES_PER_STEP * page, LANES)
    per_b3 = lambda bi, ji: (bi, 0, 0)
    keys3 = lambda bi, ji: (bi, 0, ji)
    return pl.pallas_call(
        _fox_decode_gathered_kernel,
        grid=(db, past // kb),
        in_specs=[
            pl.BlockSpec((1, ds, d), per_b3),
            pl.BlockSpec((1, ds, d), per_b3),
            pl.BlockSpec((1, page, d), per_b3),
            pl.BlockSpec((1, page, d), per_b3),
            pl.BlockSpec((1, nh, page), per_b3),
            pl.BlockSpec((1, d, kb), keys3),
            pl.BlockSpec((1, d, kb), keys3),
            pl.BlockSpec((1, nh, kb), keys3),
        ],
        out_specs=pl.BlockSpec((1, ds, d), per_b3),
        out_shape=jax.ShapeDtypeStruct((db, ds, d), F32),
        scratch_shapes=_decode_scratch(ds * nh, d),
        compiler_params=_cparams(("parallel", "arbitrary")),
        name="fox_decode_attention_gathered",
    )(q, gate, k_new, v_new, cn_bias, k_g, v_g, ck_g)


def _block_diag_ones(n, blk):
    i = jnp.arange(n)
    return (i[:, None] // blk == i[None, :] // blk).astype(BF16)


def _lower_tri(n, seq):
    i = jnp.arange(n)
    return ((i[:, None] >= i[None, :]) & (i[:, None] // seq == i[None, :] // seq)).astype(BF16)


def _prep_weights(w_ret_in, w_ret_out, w_kvf, b_f, g_k, w_fox_qg, g_q, w_fox_out, w_mlp_up, w_mlp_down):
    d = w_kvf.shape[0]
    nh = FOX_HEADS
    wf = jnp.zeros((d, LANES), BF16).at[:, :nh].set(w_kvf[:, 2 * d:].astype(BF16))
    bf = jnp.zeros((1, LANES), F32).at[0, :nh].set(b_f)
    per_layer = lambda a: [a[l].astype(BF16) for l in range(a.shape[0])]
    return dict(
        ret_in=w_ret_in.astype(BF16), ret_out=w_ret_out.astype(BF16),
        wk=w_kvf[:, :d].astype(BF16), wv=w_kvf[:, d:2 * d].T.astype(BF16), wf=wf, bf=bf,
        gk_t=jnp.tile(g_k, nh).reshape(1, d),
        wq=per_layer(w_fox_qg[:, :, :d]), wg=per_layer(w_fox_qg[:, :, d:]),
        gq_t=(jnp.tile(g_q, (1, nh)) * (FOX_DH ** -0.5 * LOG2E)).reshape(-1, 1, d),
        fox_out=w_fox_out.astype(BF16), up=w_mlp_up.astype(BF16), down=w_mlp_down.astype(BF16),
        bd=_block_diag_ones(MXU_DIM, FOX_DH),
    )


def _prompt_forward(x_prompt, meta, g_attn, g_mlp, w):
    b, seq, d = x_prompt.shape
    pad = RET_CHUNK - N_META
    x = jnp.concatenate([jnp.zeros((b, pad, d), F32),
                         jnp.broadcast_to(meta[None], (b, N_META, d)), x_prompt], axis=1)
    t = x.shape[1]
    pos = jnp.arange(t) - pad
    valid = pos >= 0
    x = x.reshape(b * t, d)
    n_ret = w["ret_in"].shape[0]
    states = None
    for l in range(n_ret):
        p = norm_matmul(x, g_attn[l], w["ret_in"], l, BF16)
        s0 = jnp.zeros((1, b, RET_HEADS, 256, 512), F32)
        og, states = retention(p, s0, 0, states, l, n_ret, RET_CHUNK, pos, valid, BF16)
        x = proj_mlp(x, og, w["ret_out"], l, g_mlp[l], w["up"], w["down"], l)

    tm = _tile(t, 512, LANES)
    kt, vt32, logf, _, ckm, kb, vtb = kv_proj(x, w["g_kv"], w["wk"], w["wv"], w["wf"], w["bf"], w["gk_t"],
                                              w["bd"], _lower_tri(tm, tm), b, pad)
    pre = RET_CHUNK
    nh = FOX_HEADS
    ck = ckm.reshape(b, t, nh // 2, 2).transpose(0, 2, 1, 3)
    kb3 = kb.reshape(b, t, d)
    vt = vtb.reshape(b, d // LANES, LANES, t)
    xr = x.reshape(b, t, d)[:, pre:].reshape(b * seq, d)
    for l in range(n_ret, g_attn.shape[0]):
        q, gate = fox_in(xr, g_attn[l], w["wq"][l - n_ret], w["wg"][l - n_ret], w["gq_t"][l - n_ret], w["bd"])
        a = fox_prompt_attention(q.reshape(b, seq, d), kb3, vt, gate.reshape(b, seq, d), ck, pre)
        xr = proj_mlp(xr, a.reshape(b * seq, d), w["fox_out"], l - n_ret, g_mlp[l], w["up"], w["down"], l)
    y = xr.reshape(b, seq, d)
    k4 = kt.reshape(b, nh, FOX_DH, t)[:, :, :, pad:].transpose(0, 3, 1, 2)
    v4 = vt32.reshape(b, nh, FOX_DH, t)[:, :, :, pad:].transpose(0, 3, 1, 2)
    return y, states, k4, v4, logf.reshape(b, t, nh)[:, pad:]


def _sample_forward(x_sample, state_ret, cache_k, cache_v, cache_logf, page_table, g_attn, g_mlp, w):
    db, ds, d = x_sample.shape
    n_pool, page, nh, dh = cache_k.shape
    past = page_table.shape[1] * page
    pos = past + jnp.arange(ds)
    valid = jnp.ones((ds,), bool)
    x = x_sample.reshape(db * ds, d)
    n_ret = w["ret_in"].shape[0]
    states = None
    for l in range(n_ret):
        p = norm_matmul(x, g_attn[l], w["ret_in"], l, F32)
        og, states = retention(p, state_ret, l, states, l, n_ret, ds, pos, valid, F32)
        x = proj_mlp(x, og, w["ret_out"], l, g_mlp[l], w["up"], w["down"], l)

    n = db * ds
    kt, vt32, logf, cum, _, kb, vtb = kv_proj(x, w["g_kv"], w["wk"], w["wv"], w["wf"], w["bf"], w["gk_t"],
                                              w["bd"], _lower_tri(n, ds), 1, 0)
    k, v = kt[0].T, vt32[0].T
    zrows = jnp.zeros((db, page - ds, d), BF16)
    k_new = jnp.concatenate([kb.reshape(db, ds, d), zrows], axis=1)
    v_new = jnp.concatenate([vtb[0].T.reshape(db, ds, d), zrows], axis=1)
    cum_t = jnp.zeros((db, nh, page), F32).at[:, :, :ds].set(cum.reshape(db, ds, nh).transpose(0, 2, 1))
    tri_u = _lower_tri(page, page).T
    ck3 = cache_k.transpose(0, 2, 3, 1).reshape(n_pool, d, page)
    cv3 = cache_v.transpose(0, 2, 3, 1).reshape(n_pool, d, page)
    clf_t = cache_logf.transpose(0, 2, 1)
    for l in range(n_ret, g_attn.shape[0]):
        q, gate = fox_in(x, g_attn[l], w["wq"][l - n_ret], w["wg"][l - n_ret], w["gq_t"][l - n_ret], w["bd"])
        q3, gate3 = q.reshape(db, ds, d), gate.reshape(db, ds, d)
        if l == n_ret:
            a, k_g, v_g, ck_g, cn_bias = fox_decode_attention_paged(
                page_table, q3, gate3, k_new, v_new, cum_t, tri_u, ck3, cv3, clf_t)
        else:
            a = fox_decode_attention_gathered(q3, gate3, k_new, v_new, cn_bias, k_g, v_g, ck_g)
        x = proj_mlp(x, a.reshape(n, d), w["fox_out"], l - n_ret, g_mlp[l], w["up"], w["down"], l)
    return (x.reshape(db, ds, d), states, k.reshape(db, ds, nh, dh), v.reshape(db, ds, nh, dh),
            logf.reshape(db, ds, nh))


def kernel(x_prompt, x_sample, state_ret, cache_k, cache_v, cache_logf, page_table, meta, g_attn, g_mlp,
           w_ret_in, w_ret_out, g_kv, w_kvf, b_f, g_k, w_fox_qg, g_q, w_fox_out, w_mlp_up, w_mlp_down):
    w = _prep_weights(w_ret_in, w_ret_out, w_kvf, b_f, g_k, w_fox_qg, g_q, w_fox_out, w_mlp_up, w_mlp_down)
    w["g_kv"] = g_kv
    y_p, s_p, k_p, v_p, lf_p = _prompt_forward(x_prompt, meta, g_attn, g_mlp, w)
    y_s, s_s, k_s, v_s, lf_s = _sample_forward(x_sample, state_ret, cache_k, cache_v, cache_logf, page_table,
                                               g_attn, g_mlp, w)
    return (y_p, y_s, s_p, s_s, k_p, v_p, lf_p, k_s, v_s, lf_s)
```

```python
import functools

import jax
import jax.numpy as jnp
from jax import lax
from jax.experimental import pallas as pl
from jax.experimental.pallas import tpu as pltpu

F32 = jnp.float32
BF16 = jnp.bfloat16

N_META = 16
RET_HEADS = 4
RET_CHUNK = 128
ROPE_BASE = 10000.0
FOX_HEADS = 16
FOX_DH = 64
EPS = 1e-6
MASK_VALUE = -1e30

LANES = 128
MXU_DIM = 256
VMEM_LIMIT = 56 * 1024 * 1024
PAGES_PER_STEP = 8
DENOM_ROWS = 16
ATTN_PAIRS_PER_STEP = 2
LOG2E = 1.4426950408889634


def _cparams(sem):
    return pltpu.CompilerParams(dimension_semantics=sem, vmem_limit_bytes=VMEM_LIMIT)


def _tile(n, target, mult=8):
    best = None
    for t in range(mult, min(n, target) + 1, mult):
        if n % t == 0:
            best = t
    assert best is not None, (n, target, mult)
    return best


def _rms(x, g_row):
    ms = jnp.mean(x * x, axis=-1, keepdims=True)
    return x * lax.rsqrt(ms + EPS) * g_row


def _split2(x):
    hi = x.astype(BF16)
    lo = (x - hi.astype(F32)).astype(BF16)
    return hi, lo


def _split3(x):
    hi = x.astype(BF16)
    r = x - hi.astype(F32)
    mid = r.astype(BF16)
    lo = (r - mid.astype(F32)).astype(BF16)
    return hi, mid, lo


def _head_rms(x, bd_ref, g_row):
    xx = x * x
    hi, lo = _split2(xx)
    bd = bd_ref[...]
    parts = []
    for c in range(x.shape[1] // MXU_DIM):
        sl = slice(c * MXU_DIM, (c + 1) * MXU_DIM)
        parts.append(jnp.dot(hi[:, sl], bd, preferred_element_type=F32)
                     + jnp.dot(lo[:, sl], bd, preferred_element_type=F32))
    ss = jnp.concatenate(parts, axis=-1)
    return x * lax.rsqrt(ss * (1.0 / FOX_DH) + EPS) * g_row


def _norm_matmul_kernel(x_ref, g_ref, w_ref, o_ref, xn_ref):
    @pl.when(pl.program_id(1) == 0)
    def _():
        xn_ref[...] = _rms(x_ref[...], g_ref[...]).astype(BF16)

    o_ref[...] = jnp.dot(xn_ref[...], w_ref[...], preferred_element_type=F32).astype(o_ref.dtype)


def _norm_matmul_resident_kernel(x_ref, g_ref, w_ref, o_ref):
    xn = _rms(x_ref[...], g_ref[...]).astype(BF16)
    o_ref[...] = jnp.dot(xn, w_ref[...], preferred_element_type=F32).astype(o_ref.dtype)


def _norm_matmul_resident(x, g, w_all, layer, out_dtype):
    n, d = x.shape
    nout = w_all.shape[2]
    tm = _tile(n, 384, 16)
    return pl.pallas_call(
        _norm_matmul_resident_kernel,
        grid=(n // tm,),
        in_specs=[
            pl.BlockSpec((tm, d), lambda i: (i, 0)),
            pl.BlockSpec((1, d), lambda i: (0, 0)),
            pl.BlockSpec((None, d, nout), lambda i: (layer, 0, 0)),
        ],
        out_specs=pl.BlockSpec((tm, nout), lambda i: (i, 0)),
        out_shape=jax.ShapeDtypeStruct((n, nout), out_dtype),
        compiler_params=_cparams(("parallel",)),
        name="norm_matmul_resident",
    )(x, g.reshape(1, d), w_all)


def norm_matmul(x, g, w_all, layer, out_dtype):
    n, d = x.shape
    nout = w_all.shape[2]
    if n >= 4096:
        return _norm_matmul_resident(x, g, w_all, layer, out_dtype)
    tm = _tile(n, 1536)
    tn = _tile(nout, 1024, LANES)
    return pl.pallas_call(
        _norm_matmul_kernel,
        grid=(n // tm, nout // tn),
        in_specs=[
            pl.BlockSpec((tm, d), lambda i, j: (i, 0)),
            pl.BlockSpec((1, d), lambda i, j: (0, 0)),
            pl.BlockSpec((None, d, tn), lambda i, j: (layer, 0, j)),
        ],
        out_specs=pl.BlockSpec((tm, tn), lambda i, j: (i, j)),
        out_shape=jax.ShapeDtypeStruct((n, nout), out_dtype),
        scratch_shapes=[pltpu.VMEM((tm, d), BF16)],
        compiler_params=_cparams(("parallel", "arbitrary")),
        name="norm_matmul",
    )(x, g.reshape(1, d), w_all)


def _retention_kernel(q_ref, k_ref, v_ref, gt_ref, cq_ref, sq_ref, ck_ref, sk_ref,
                      dm_ref, qd_ref, kd_ref, gc_ref, s0_ref, *rest):
    og_ref, s_ref = rest[-2:]
    @pl.when(pl.program_id(1) == 0)
    def _():
        s_ref[0] = s0_ref[0, 0]

    nh, dk, dv = s_ref.shape[1:]
    half = dk // 2
    cq, sq = cq_ref[...], sq_ref[...]
    ck, sk = ck_ref[...], sk_ref[...]
    for h in range(nh):
        q1 = q_ref[:, h * dk:h * dk + half].astype(F32)
        q2 = q_ref[:, h * dk + half:(h + 1) * dk].astype(F32)
        k1 = k_ref[:, h * dk:h * dk + half].astype(F32)
        k2 = k_ref[:, h * dk + half:(h + 1) * dk].astype(F32)
        qr = jnp.concatenate([q1 * cq - q2 * sq, q2 * cq + q1 * sq], axis=-1)
        kr = jnp.concatenate([k1 * ck - k2 * sk, k2 * ck + k1 * sk], axis=-1)
        qb = qr.astype(BF16)
        kb = kr.astype(BF16)
        vb = v_ref[:, h * dv:(h + 1) * dv].astype(BF16)
        s_old = s_ref[0, h]

        scores = lax.dot_general(qb, kb, (((1,), (1,)), ((), ())),
                                 preferred_element_type=F32) * dm_ref[h]
        intra = jnp.dot(scores.astype(BF16), vb, preferred_element_type=F32)
        cross = jnp.dot(qb, s_old.astype(BF16), preferred_element_type=F32) * qd_ref[h]
        o = intra + cross

        kd = (kr * kd_ref[h]).astype(BF16)
        s_ref[0, h] = gc_ref[h] * s_old + lax.dot_general(
            kd, vb, (((0,), (0,)), ((), ())), preferred_element_type=F32)

        on = o * lax.rsqrt(jnp.mean(o * o, axis=-1, keepdims=True) + EPS)
        gt = gt_ref[:, h * dv:(h + 1) * dv].astype(F32)
        og_ref[:, h * dv:(h + 1) * dv] = (gt * jax.nn.sigmoid(gt) * on).astype(og_ref.dtype)


def _retention_tables(chunk, pos, valid):
    dk = 256
    half = dk // 2
    lg = jnp.log1p(-jnp.exp2(-5.0 - jnp.arange(RET_HEADS, dtype=F32)))
    idx = jnp.arange(chunk, dtype=F32)
    diff = idx[:, None] - idx[None, :]
    dmat = jnp.where(diff >= 0, jnp.exp(lg[:, None, None] * jnp.maximum(diff, 0.0)), 0.0)
    qdec = jnp.exp(lg[:, None] * (idx[None, :] + 1.0))[:, :, None]
    kdec = jnp.exp(lg[:, None] * (chunk - 1.0 - idx[None, :]))[:, :, None]
    gc = jnp.exp(lg * chunk)[:, None, None]
    inv_freq = ROPE_BASE ** (-jnp.arange(half, dtype=F32) / half)
    ang = pos.astype(F32)[:, None] * inv_freq[None, :]
    cos, sin = jnp.cos(ang), jnp.sin(ang)
    kscale = (dk ** -0.5) * valid.astype(F32)[:, None]
    return dmat, qdec, kdec, gc, cos, sin, cos * kscale, sin * kscale


def retention(p, s0_all, s0_layer, states, layer, n_layers, chunk, pos, valid, out_dtype):
    _, b, h, dk, dv = s0_all.shape
    n = p.shape[0]
    t = n // b
    nc = t // chunk
    dmat, qdec, kdec, gc, cq, sq, ck, sk = _retention_tables(chunk, pos, valid)
    row = lambda bi, ci: bi * nc + ci
    kblk = (h * dk) // (h * dk)
    vblk = (2 * h * dk) // (h * dv)
    gblk = (2 * h * dk + h * dv) // (h * dv)
    rope_spec = pl.BlockSpec((chunk, dk // 2), lambda bi, ci: (ci, 0))
    full3 = lambda bi, ci: (0, 0, 0)
    carried = [] if states is None else [states]
    og, new_states = pl.pallas_call(
        _retention_kernel,
        grid=(b, nc),
        in_specs=[
            pl.BlockSpec((chunk, h * dk), lambda bi, ci: (row(bi, ci), 0)),
            pl.BlockSpec((chunk, h * dk), lambda bi, ci: (row(bi, ci), kblk)),
            pl.BlockSpec((chunk, h * dv), lambda bi, ci: (row(bi, ci), vblk)),
            pl.BlockSpec((chunk, h * dv), lambda bi, ci: (row(bi, ci), gblk)),
            rope_spec, rope_spec, rope_spec, rope_spec,
            pl.BlockSpec((h, chunk, chunk), full3),
            pl.BlockSpec((h, chunk, 1), full3),
            pl.BlockSpec((h, chunk, 1), full3),
            pl.BlockSpec((h, 1, 1), full3),
            pl.BlockSpec((1, 1, h, dk, dv), lambda bi, ci: (s0_layer, bi, 0, 0, 0)),
        ] + [pl.BlockSpec(memory_space=pl.ANY)] * len(carried),
        out_specs=[
            pl.BlockSpec((chunk, h * dv), lambda bi, ci: (row(bi, ci), 0)),
            pl.BlockSpec((None, 1, h, dk, dv), lambda bi, ci: (layer, bi, 0, 0, 0)),
        ],
        out_shape=[
            jax.ShapeDtypeStruct((n, h * dv), out_dtype),
            jax.ShapeDtypeStruct((n_layers, b, h, dk, dv), F32),
        ],
        input_output_aliases={13: 1} if carried else {},
        compiler_params=_cparams(("parallel", "arbitrary")),
        name="retention",
    )(p, p, p, p, cq, sq, ck, sk, dmat, qdec, kdec, gc, s0_all, *carried)
    return og, new_states


def _proj_mlp_kernel(x_ref, a_ref, wo_ref, g_ref, wu_ref, wd_ref, o_ref, xn_ref):
    @pl.when(pl.program_id(1) == 0)
    def _():
        x1 = x_ref[...] + jnp.dot(a_ref[...].astype(BF16), wo_ref[...], preferred_element_type=F32)
        o_ref[...] = x1
        xn_ref[...] = _rms(x1, g_ref[...]).astype(BF16)

    hdn = jnp.dot(xn_ref[...], wu_ref[...], preferred_element_type=F32)
    hdn = jnp.square(jnp.maximum(hdn, 0.0)).astype(BF16)
    o_ref[...] += jnp.dot(hdn, wd_ref[...], preferred_element_type=F32)


def proj_mlp(x, a, wo_all, lo, g, wu_all, wd_all, lm):
    n, d = x.shape
    ka = a.shape[1]
    dff = wu_all.shape[2]
    tm = _tile(n, 1056, 16)
    tf = _tile(dff, 1024, LANES)
    return pl.pallas_call(
        _proj_mlp_kernel,
        grid=(n // tm, dff // tf),
        in_specs=[
            pl.BlockSpec((tm, d), lambda i, f: (i, 0)),
            pl.BlockSpec((tm, ka), lambda i, f: (i, 0)),
            pl.BlockSpec((None, ka, d), lambda i, f: (lo, 0, 0)),
            pl.BlockSpec((1, d), lambda i, f: (0, 0)),
            pl.BlockSpec((None, d, tf), lambda i, f: (lm, 0, f)),
            pl.BlockSpec((None, tf, d), lambda i, f: (lm, f, 0)),
        ],
        out_specs=pl.BlockSpec((tm, d), lambda i, f: (i, 0)),
        out_shape=jax.ShapeDtypeStruct((n, d), F32),
        scratch_shapes=[pltpu.VMEM((tm, d), BF16)],
        compiler_params=_cparams(("parallel", "arbitrary")),
        name="proj_mlp",
    )(x, a, wo_all, g.reshape(1, d), wu_all, wd_all)


def _kv_kernel(x_ref, g_ref, wk_ref, wv_ref, wf_ref, bf_ref, gk_ref, bd_ref, tri_ref,
               kt_ref, vt_ref, lf_ref, cum_ref, ckm_ref, kb_ref, vtb_ref, carry_ref, *, pad):
    t = pl.program_id(1)
    tm = x_ref.shape[0]

    @pl.when(t == 0)
    def _():
        carry_ref[...] = jnp.zeros_like(carry_ref)

    xn = _rms(x_ref[...], g_ref[...]).astype(BF16)
    kraw = jnp.dot(xn, wk_ref[...], preferred_element_type=F32)
    k = _head_rms(kraw, bd_ref, gk_ref[...])
    vt = lax.dot_general(wv_ref[...], xn, (((1,), (1,)), ((), ())), preferred_element_type=F32)
    kt_ref[0] = k.T
    vt_ref[0] = vt
    kb_ref[...] = k.astype(BF16)
    vtb_ref[0] = vt.astype(BF16)

    z = jnp.dot(xn, wf_ref[...], preferred_element_type=F32) + bf_ref[...]
    logf = jnp.minimum(z, 0.0) - jnp.log1p(jnp.exp(-jnp.abs(z)))
    rows = t * tm + lax.broadcasted_iota(jnp.int32, (tm, LANES), 0)
    valid = rows >= pad
    logf = jnp.where(valid, logf, 0.0)
    hi, mid, lo = _split3(logf)
    tri = tri_ref[...]
    cum = carry_ref[...] + (jnp.dot(tri, hi, preferred_element_type=F32)
                            + jnp.dot(tri, mid, preferred_element_type=F32)
                            + jnp.dot(tri, lo, preferred_element_type=F32))
    carry_ref[...] = cum[tm - 1:tm, :]
    nh = lf_ref.shape[1]
    lf_ref[...] = logf[:, :nh]
    cum_ref[...] = cum[:, :nh]
    ckm_ref[...] = jnp.where(valid, cum * LOG2E, -MASK_VALUE)[:, :nh]


def kv_proj(x, g, wk, wv, wf, bf, gk_t, bd, tri, nb, pad):
    n, d = x.shape
    tm = tri.shape[0]
    tb = n // nb
    nt = max(tb // tm, 1)
    nbg = n // (tm * nt)
    nh = FOX_HEADS
    row = lambda bi, ti: (bi * nt + ti, 0)
    col = lambda bi, ti: (bi, 0, ti)
    full = lambda bi, ti: (0, 0)
    return pl.pallas_call(
        functools.partial(_kv_kernel, pad=pad),
        grid=(nbg, nt),
        in_specs=[
            pl.BlockSpec((tm, d), row),
            pl.BlockSpec((1, d), full),
            pl.BlockSpec((d, d), full),
            pl.BlockSpec((d, d), full),
            pl.BlockSpec((d, LANES), full),
            pl.BlockSpec((1, LANES), full),
            pl.BlockSpec((1, d), full),
            pl.BlockSpec((MXU_DIM, MXU_DIM), full),
            pl.BlockSpec((tm, tm), full),
        ],
        out_specs=[
            pl.BlockSpec((1, d, tm), col),
            pl.BlockSpec((1, d, tm), col),
            pl.BlockSpec((tm, nh), row),
            pl.BlockSpec((tm, nh), row),
            pl.BlockSpec((tm, nh), row),
            pl.BlockSpec((tm, d), row),
            pl.BlockSpec((1, d, tm), col),
        ],
        out_shape=[
            jax.ShapeDtypeStruct((nbg, d, nt * tm), F32),
            jax.ShapeDtypeStruct((nbg, d, nt * tm), F32),
            jax.ShapeDtypeStruct((n, nh), F32),
            jax.ShapeDtypeStruct((n, nh), F32),
            jax.ShapeDtypeStruct((n, nh), F32),
            jax.ShapeDtypeStruct((n, d), BF16),
            jax.ShapeDtypeStruct((nbg, d, nt * tm), BF16),
        ],
        scratch_shapes=[pltpu.VMEM((1, LANES), F32)],
        compiler_params=_cparams(("parallel", "arbitrary")),
        name="kv_proj",
    )(x, g.reshape(1, d), wk, wv, wf, bf, gk_t, bd, tri)


def _fox_in_kernel(x_ref, g_ref, wq_ref, wg_ref, gq_ref, bd_ref, q_ref, gate_ref):
    xn = _rms(x_ref[...], g_ref[...]).astype(BF16)
    qraw = jnp.dot(xn, wq_ref[...], preferred_element_type=F32)
    q_ref[...] = _head_rms(qraw, bd_ref, gq_ref[...]).astype(BF16)
    gate_ref[...] = jnp.dot(xn, wg_ref[...], preferred_element_type=F32).astype(gate_ref.dtype)


def fox_in(x, g, wq, wg, gq_t, bd):
    n, d = x.shape
    tm = _tile(n, 1024, 16)
    row = lambda i: (i, 0)
    full = lambda i: (0, 0)
    return pl.pallas_call(
        _fox_in_kernel,
        grid=(n // tm,),
        in_specs=[
            pl.BlockSpec((tm, d), row),
            pl.BlockSpec((1, d), full),
            pl.BlockSpec((d, d), full),
            pl.BlockSpec((d, d), full),
            pl.BlockSpec((1, d), full),
            pl.BlockSpec((MXU_DIM, MXU_DIM), full),
        ],
        out_specs=[pl.BlockSpec((tm, d), row), pl.BlockSpec((tm, d), row)],
        out_shape=[jax.ShapeDtypeStruct((n, d), BF16), jax.ShapeDtypeStruct((n, d), BF16)],
        compiler_params=_cparams(("parallel",)),
        name="fox_in",
    )(x, g.reshape(1, d), wq, wg, gq_t, bd)


def _fox_prompt_kernel(q_ref, k_ref, vt_ref, g_ref, ck_ref, o_ref, m_ref, acc_ref, sa_ref, sb_ref, *, pre):
    i = pl.program_id(2)
    tq = q_ref.shape[1]
    tk = tq
    npair = vt_ref.shape[1]
    low = lax.broadcasted_iota(jnp.int32, (tq, LANES), 1) < FOX_DH
    qs = []
    for e in range(npair):
        q = q_ref[0, :, e * LANES:(e + 1) * LANES]
        zero = jnp.zeros_like(q)
        qs.append(jnp.concatenate([jnp.where(low, q, zero), jnp.where(low, zero, q)], axis=0))

    m_ref[...] = jnp.full_like(m_ref, MASK_VALUE)
    acc_ref[...] = jnp.zeros_like(acc_ref)

    def scores(e, j0, width):
        kj = k_ref[0, pl.ds(j0, width), e * LANES:(e + 1) * LANES]
        s = lax.dot_general(kj, qs[e], (((1,), (1,)), ((), ())), preferred_element_type=F32)
        ck = ck_ref[0, e, pl.ds(j0, width), :]
        return s[:, :tq] - ck[:, 0:1], s[:, tq:] - ck[:, 1:2]

    def start(b):
        return pl.multiple_of(pre + b * tk, LANES)

    def qk(b, s_ref):
        for e in range(npair):
            s0, s1 = scores(e, start(b), tk)
            s_ref[e, :, :tq] = s0
            s_ref[e, :, tq:] = s1

    def soft(e, s, vtj):
        m_old = m_ref[e]
        m_new = jnp.maximum(m_old, jnp.max(s, axis=0, keepdims=True))
        p = jnp.exp2(s - m_new).astype(BF16)
        alpha = jnp.exp2(m_old - m_new)
        vt_ones = jnp.concatenate([vtj, jnp.ones((DENOM_ROWS, vtj.shape[1]), BF16)], axis=0)
        acc_ref[e] = alpha * acc_ref[e] + jnp.dot(vt_ones, p, preferred_element_type=F32)
        m_ref[e] = m_new

    def soft_block(b, s_ref):
        for e in range(npair):
            soft(e, s_ref[e], vt_ref[0, e, :, pl.ds(start(b), tk)])

    odd = (i & 1) == 1

    @pl.when(odd)
    def _():
        qk(0, sb_ref)
        qk(1, sa_ref)
        soft_block(0, sb_ref)

    @pl.when(jnp.logical_not(odd))
    def _():
        qk(0, sa_ref)

    base = i & 1

    def body(jj, carry):
        b0 = base + 2 * jj
        qk(b0 + 1, sb_ref)
        soft_block(b0, sa_ref)
        qk(b0 + 2, sa_ref)
        soft_block(b0 + 1, sb_ref)
        return carry

    lax.fori_loop(0, i >> 1, body, 0)

    key = lax.broadcasted_iota(jnp.int32, (tk, tq), 0)
    qry = lax.broadcasted_iota(jnp.int32, (tk, tq), 1)
    ok = key <= qry
    top = lax.broadcasted_iota(jnp.int32, (LANES, tq), 0) < FOX_DH
    for e in range(npair):
        sd0 = jnp.where(ok, sa_ref[e, :, :tq], MASK_VALUE)
        sd1 = jnp.where(ok, sa_ref[e, :, tq:], MASK_VALUE)
        sp0, sp1 = scores(e, 0, pre)
        s_last = jnp.concatenate([jnp.concatenate([sp0, sd0], axis=0), jnp.concatenate([sp1, sd1], axis=0)], axis=1)
        vt_last = jnp.concatenate([vt_ref[0, e, :, 0:pre], vt_ref[0, e, :, pl.ds(start(i), tk)]], axis=1)
        soft(e, s_last, vt_last)
    for e in range(npair):
        ot = acc_ref[e, 0:LANES, :] / acc_ref[e, LANES:LANES + 1, :]
        o = jnp.where(top, ot[:, :tq], ot[:, tq:]).T
        sl = slice(e * LANES, (e + 1) * LANES)
        o_ref[0, :, sl] = (o * jax.nn.sigmoid(g_ref[0, :, sl].astype(F32))).astype(o_ref.dtype)


def fox_prompt_attention(q, kb, vt, gate, ck, pre):
    b, tq_all, d = q.shape
    tk_all = kb.shape[1]
    npair = ATTN_PAIRS_PER_STEP
    w = npair * LANES
    tq = _tile(tq_all, 512, LANES)
    return pl.pallas_call(
        functools.partial(_fox_prompt_kernel, pre=pre),
        grid=(b, d // w, tq_all // tq),
        in_specs=[
            pl.BlockSpec((1, tq, w), lambda bi, hi, qi: (bi, qi, hi)),
            pl.BlockSpec((1, tk_all, w), lambda bi, hi, qi: (bi, 0, hi)),
            pl.BlockSpec((1, npair, LANES, tk_all), lambda bi, hi, qi: (bi, hi, 0, 0)),
            pl.BlockSpec((1, tq, w), lambda bi, hi, qi: (bi, qi, hi)),
            pl.BlockSpec((1, npair, tk_all, 2), lambda bi, hi, qi: (bi, hi, 0, 0)),
        ],
        out_specs=pl.BlockSpec((1, tq, w), lambda bi, hi, qi: (bi, qi, hi)),
        out_shape=jax.ShapeDtypeStruct((b, tq_all, d), BF16),
        scratch_shapes=[
            pltpu.VMEM((npair, 1, 2 * tq), F32),
            pltpu.VMEM((npair, LANES + DENOM_ROWS, 2 * tq), F32),
            pltpu.VMEM((npair, tq, 2 * tq), F32),
            pltpu.VMEM((npair, tq, 2 * tq), F32),
        ],
        compiler_params=_cparams(("parallel", "parallel", "arbitrary")),
        name="fox_prompt_attention",
    )(q, kb, vt, gate, ck)


def _decode_head_mask(shape, row_axis, lane_axis):
    hrow = lax.broadcasted_iota(jnp.int32, shape, row_axis) & (FOX_HEADS - 1)
    hlane = lax.broadcasted_iota(jnp.int32, shape, lane_axis) >> 6
    return hrow == hlane


def _decode_init(q_ref, qbd_ref, m_ref, l_ref, acc_ref):
    ds, d = q_ref.shape[1:]
    q = q_ref[0].astype(F32)
    qrep = jnp.concatenate([jnp.broadcast_to(q[qi:qi + 1, :], (FOX_HEADS, d)) for qi in range(ds)], axis=0)
    qbd_ref[...] = jnp.where(_decode_head_mask(qrep.shape, 0, 1), qrep, 0.0).astype(BF16)
    m_ref[...] = jnp.full_like(m_ref, MASK_VALUE)
    l_ref[...] = jnp.zeros_like(l_ref)
    acc_ref[...] = jnp.zeros_like(acc_ref)


def _decode_update(m_ref, l_ref, acc_ref, s, v_bf, v_key_axis):
    m_old = m_ref[...]
    m_new = jnp.maximum(m_old, jnp.max(s, axis=-1, keepdims=True))
    p = jnp.exp2(s - m_new)
    alpha = jnp.exp2(m_old - m_new)
    l_ref[...] = alpha * l_ref[...] + jnp.sum(p, axis=-1, keepdims=True)
    acc_ref[...] = alpha * acc_ref[...] + lax.dot_general(
        p.astype(BF16), v_bf, (((1,), (v_key_axis,)), ((), ())), preferred_element_type=F32)
    m_ref[...] = m_new


def _decode_finish(qbd_ref, kn_ref, vn_ref, cn, g_ref, o_ref, m_ref, l_ref, acc_ref):
    ds, d = g_ref.shape[1:]
    nh = FOX_HEADS
    nr, page = ds * nh, kn_ref.shape[1]
    sn = lax.dot_general(qbd_ref[...], kn_ref[0], (((1,), (1,)), ((), ())),
                         preferred_element_type=F32) - jnp.concatenate([cn] * ds, axis=0)
    qi = lax.broadcasted_iota(jnp.int32, (nr, page), 0) >> 4
    kj = lax.broadcasted_iota(jnp.int32, (nr, page), 1)
    _decode_update(m_ref, l_ref, acc_ref, jnp.where(kj <= qi, sn, MASK_VALUE), vn_ref[0], 0)
    acc = acc_ref[...] / l_ref[...]
    acc = jnp.where(_decode_head_mask((nr, d), 0, 1), acc, 0.0)
    o = jnp.concatenate(
        [jnp.sum(acc[qi_ * nh:(qi_ + 1) * nh, :], axis=0, keepdims=True) for qi_ in range(ds)], axis=0)
    o_ref[0] = (o * jax.nn.sigmoid(g_ref[0].astype(F32))).astype(o_ref.dtype)


def _fox_decode_paged_kernel(pt_ref, q_ref, g_ref, kn_ref, vn_ref, cn_ref, tri_ref, *rest, pg):
    k_refs = rest[:pg]
    v_refs = rest[pg:2 * pg]
    lf_refs = rest[2 * pg:3 * pg]
    o_ref, kg_ref, vg_ref, ckg_ref, cng_ref = rest[3 * pg:3 * pg + 5]
    qbd_ref, m_ref, l_ref, acc_ref, carry_ref = rest[3 * pg + 5:]
    j = pl.program_id(1)
    nh = FOX_HEADS
    ds = q_ref.shape[1]
    page = k_refs[0].shape[2]

    @pl.when(j == 0)
    def _():
        _decode_init(q_ref, qbd_ref, m_ref, l_ref, acc_ref)
        carry_ref[...] = jnp.zeros_like(carry_ref)

    parts = []
    for pi in range(pg):
        kg_ref[0, :, pi * page:(pi + 1) * page] = k_refs[pi][0].astype(BF16)
        vg_ref[0, :, pi * page:(pi + 1) * page] = v_refs[pi][0].astype(BF16)
        parts.extend(_split3(lf_refs[pi][0]))
    w = jnp.dot(jnp.concatenate(parts, axis=0), tri_ref[...], preferred_element_type=F32)
    carry = carry_ref[...]
    cums = []
    for pi in range(pg):
        base = 3 * pi * nh
        wp = w[base:base + nh] + w[base + nh:base + 2 * nh] + w[base + 2 * nh:base + 3 * nh]
        cp = carry + wp
        cums.append(cp)
        carry = cp[:, page - 1:page]
    carry_ref[...] = carry
    ck = jnp.concatenate(cums, axis=-1) * LOG2E
    ckg_ref[0] = ck

    s = jnp.dot(qbd_ref[...], kg_ref[0], preferred_element_type=F32) - jnp.concatenate([ck] * ds, axis=0)
    _decode_update(m_ref, l_ref, acc_ref, s, vg_ref[0], 1)

    @pl.when(j == pl.num_programs(1) - 1)
    def _():
        cn = (carry_ref[...] + cn_ref[0]) * LOG2E
        cng_ref[0] = cn
        _decode_finish(qbd_ref, kn_ref, vn_ref, cn, g_ref, o_ref, m_ref, l_ref, acc_ref)


def _fox_decode_gathered_kernel(q_ref, g_ref, kn_ref, vn_ref, cng_ref, kg_ref, vg_ref, ckg_ref, o_ref,
                                qbd_ref, m_ref, l_ref, acc_ref):
    j = pl.program_id(1)
    ds = q_ref.shape[1]

    @pl.when(j == 0)
    def _():
        _decode_init(q_ref, qbd_ref, m_ref, l_ref, acc_ref)

    s = jnp.dot(qbd_ref[...], kg_ref[0], preferred_element_type=F32) - jnp.concatenate([ckg_ref[0]] * ds, axis=0)
    _decode_update(m_ref, l_ref, acc_ref, s, vg_ref[0], 1)

    @pl.when(j == pl.num_programs(1) - 1)
    def _():
        _decode_finish(qbd_ref, kn_ref, vn_ref, cng_ref[0], g_ref, o_ref, m_ref, l_ref, acc_ref)


def _decode_scratch(nr, d):
    return [pltpu.VMEM((nr, d), BF16), pltpu.VMEM((nr, 1), F32), pltpu.VMEM((nr, 1), F32), pltpu.VMEM((nr, d), F32)]


def fox_decode_attention_paged(page_table, q, gate, k_new, v_new, cum_new_t, tri_u, cache_k, cache_v, cache_lf_t):
    db, ds, d = q.shape
    npages = page_table.shape[1]
    page = cache_k.shape[2]
    pg = _tile(npages, PAGES_PER_STEP, 1)
    nh = FOX_HEADS
    past = npages * page

    def page_spec(shape, pi):
        return pl.BlockSpec(shape, lambda bi, ji, pt: (pt[bi, ji * pg + pi], 0, 0))

    per_b3 = lambda bi, ji, pt: (bi, 0, 0)
    keys3 = lambda bi, ji, pt: (bi, 0, ji)
    in_specs = [
        pl.BlockSpec((1, ds, d), per_b3),
        pl.BlockSpec((1, ds, d), per_b3),
        pl.BlockSpec((1, page, d), per_b3),
        pl.BlockSpec((1, page, d), per_b3),
        pl.BlockSpec((1, nh, page), per_b3),
        pl.BlockSpec((page, page), lambda bi, ji, pt: (0, 0)),
    ]
    in_specs += [page_spec((1, d, page), pi) for pi in range(pg)]
    in_specs += [page_spec((1, d, page), pi) for pi in range(pg)]
    in_specs += [page_spec((1, nh, page), pi) for pi in range(pg)]
    grid_spec = pltpu.PrefetchScalarGridSpec(
        num_scalar_prefetch=1,
        grid=(db, npages // pg),
        in_specs=in_specs,
        out_specs=[
            pl.BlockSpec((1, ds, d), per_b3),
            pl.BlockSpec((1, d, pg * page), keys3),
            pl.BlockSpec((1, d, pg * page), keys3),
            pl.BlockSpec((1, nh, pg * page), keys3),
            pl.BlockSpec((1, nh, page), per_b3),
        ],
        scratch_shapes=_decode_scratch(ds * nh, d) + [pltpu.VMEM((nh, 1), F32)],
    )
    return pl.pallas_call(
        functools.partial(_fox_decode_paged_kernel, pg=pg),
        grid_spec=grid_spec,
        out_shape=[
            jax.ShapeDtypeStruct((db, ds, d), F32),
            jax.ShapeDtypeStruct((db, d, past), BF16),
            jax.ShapeDtypeStruct((db, d, past), BF16),
            jax.ShapeDtypeStruct((db, nh, past), F32),
            jax.ShapeDtypeStruct((db, nh, page), F32),
        ],
        compiler_params=_cparams(("parallel", "arbitrary")),
        name="fox_decode_attention_paged",
    )(page_table, q, gate, k_new, v_new, cum_new_t, tri_u,
      *([cache_k] * pg), *([cache_v] * pg), *([cache_lf_t] * pg))


def fox_decode_attention_gathered(q, gate, k_new, v_new, cn_bias, k_g, v_g, ck_g):
    db, ds, d = q.shape
    past = k_g.shape[2]
    page = k_new.shape[1]
    nh = FOX_HEADS
    kb = _tile(past, 2 * PAGES_PER_STEP * page, LANES)
    per_b3 = lambda bi, ji: (bi, 0, 0)
    keys3 = lambda bi, ji: (bi, 0, ji)
    return pl.pallas_call(
        _fox_decode_gathered_kernel,
        grid=(db, past // kb),
        in_specs=[
            pl.BlockSpec((1, ds, d), per_b3),
            pl.BlockSpec((1, ds, d), per_b3),
            pl.BlockSpec((1, page, d), per_b3),
            pl.BlockSpec((1, page, d), per_b3),
            pl.BlockSpec((1, nh, page), per_b3),
            pl.BlockSpec((1, d, kb), keys3),
            pl.BlockSpec((1, d, kb), keys3),
            pl.BlockSpec((1, nh, kb), keys3),
        ],
        out_specs=pl.BlockSpec((1, ds, d), per_b3),
        out_shape=jax.ShapeDtypeStruct((db, ds, d), F32),
        scratch_shapes=_decode_scratch(ds * nh, d),
        compiler_params=_cparams(("parallel", "arbitrary")),
        name="fox_decode_attention_gathered",
    )(q, gate, k_new, v_new, cn_bias, k_g, v_g, ck_g)


def _block_diag_ones(n, blk):
    i = jnp.arange(n)
    return (i[:, None] // blk == i[None, :] // blk).astype(BF16)


def _lower_tri(n, seq):
    i = jnp.arange(n)
    return ((i[:, None] >= i[None, :]) & (i[:, None] // seq == i[None, :] // seq)).astype(BF16)


def _prep_weights(w_ret_in, w_ret_out, w_kvf, b_f, g_k, w_fox_qg, g_q, w_fox_out, w_mlp_up, w_mlp_down):
    d = w_kvf.shape[0]
    nh = FOX_HEADS
    wf = jnp.zeros((d, LANES), BF16).at[:, :nh].set(w_kvf[:, 2 * d:].astype(BF16))
    bf = jnp.zeros((1, LANES), F32).at[0, :nh].set(b_f)
    per_layer = lambda a: [a[l].astype(BF16) for l in range(a.shape[0])]
    return dict(
        ret_in=w_ret_in.astype(BF16), ret_out=w_ret_out.astype(BF16),
        wk=w_kvf[:, :d].astype(BF16), wv=w_kvf[:, d:2 * d].T.astype(BF16), wf=wf, bf=bf,
        gk_t=jnp.tile(g_k, nh).reshape(1, d),
        wq=per_layer(w_fox_qg[:, :, :d]), wg=per_layer(w_fox_qg[:, :, d:]),
        gq_t=(jnp.tile(g_q, (1, nh)) * (FOX_DH ** -0.5 * LOG2E)).reshape(-1, 1, d),
        fox_out=w_fox_out.astype(BF16), up=w_mlp_up.astype(BF16), down=w_mlp_down.astype(BF16),
        bd=_block_diag_ones(MXU_DIM, FOX_DH),
    )


def _prompt_forward(x_prompt, meta, g_attn, g_mlp, w):
    b, seq, d = x_prompt.shape
    pad = RET_CHUNK - N_META
    x = jnp.concatenate([jnp.zeros((b, pad, d), F32),
                         jnp.broadcast_to(meta[None], (b, N_META, d)), x_prompt], axis=1)
    t = x.shape[1]
    pos = jnp.arange(t) - pad
    valid = pos >= 0
    x = x.reshape(b * t, d)
    n_ret = w["ret_in"].shape[0]
    states = None
    for l in range(n_ret):
        p = norm_matmul(x, g_attn[l], w["ret_in"], l, BF16)
        s0 = jnp.zeros((1, b, RET_HEADS, 256, 512), F32)
        og, states = retention(p, s0, 0, states, l, n_ret, RET_CHUNK, pos, valid, BF16)
        x = proj_mlp(x, og, w["ret_out"], l, g_mlp[l], w["up"], w["down"], l)

    tm = _tile(t, 512, LANES)
    kt, vt32, logf, _, ckm, kb, vtb = kv_proj(x, w["g_kv"], w["wk"], w["wv"], w["wf"], w["bf"], w["gk_t"],
                                              w["bd"], _lower_tri(tm, tm), b, pad)
    pre = RET_CHUNK
    nh = FOX_HEADS
    ck = ckm.reshape(b, t, nh // 2, 2).transpose(0, 2, 1, 3)
    kb3 = kb.reshape(b, t, d)
    vt = vtb.reshape(b, d // LANES, LANES, t)
    xr = x.reshape(b, t, d)[:, pre:].reshape(b * seq, d)
    for l in range(n_ret, g_attn.shape[0]):
        q, gate = fox_in(xr, g_attn[l], w["wq"][l - n_ret], w["wg"][l - n_ret], w["gq_t"][l - n_ret], w["bd"])
        a = fox_prompt_attention(q.reshape(b, seq, d), kb3, vt, gate.reshape(b, seq, d), ck, pre)
        xr = proj_mlp(xr, a.reshape(b * seq, d), w["fox_out"], l - n_ret, g_mlp[l], w["up"], w["down"], l)
    y = xr.reshape(b, seq, d)
    k4 = kt.reshape(b, nh, FOX_DH, t)[:, :, :, pad:].transpose(0, 3, 1, 2)
    v4 = vt32.reshape(b, nh, FOX_DH, t)[:, :, :, pad:].transpose(0, 3, 1, 2)
    return y, states, k4, v4, logf.reshape(b, t, nh)[:, pad:]


def _sample_forward(x_sample, state_ret, cache_k, cache_v, cache_logf, page_table, g_attn, g_mlp, w):
    db, ds, d = x_sample.shape
    n_pool, page, nh, dh = cache_k.shape
    past = page_table.shape[1] * page
    pos = past + jnp.arange(ds)
    valid = jnp.ones((ds,), bool)
    x = x_sample.reshape(db * ds, d)
    n_ret = w["ret_in"].shape[0]
    states = None
    for l in range(n_ret):
        p = norm_matmul(x, g_attn[l], w["ret_in"], l, F32)
        og, states = retention(p, state_ret, l, states, l, n_ret, ds, pos, valid, F32)
        x = proj_mlp(x, og, w["ret_out"], l, g_mlp[l], w["up"], w["down"], l)

    n = db * ds
    kt, vt32, logf, cum, _, kb, vtb = kv_proj(x, w["g_kv"], w["wk"], w["wv"], w["wf"], w["bf"], w["gk_t"],
                                              w["bd"], _lower_tri(n, ds), 1, 0)
    k, v = kt[0].T, vt32[0].T
    zrows = jnp.zeros((db, page - ds, d), BF16)
    k_new = jnp.concatenate([kb.reshape(db, ds, d), zrows], axis=1)
    v_new = jnp.concatenate([vtb[0].T.reshape(db, ds, d), zrows], axis=1)
    cum_t = jnp.zeros((db, nh, page), F32).at[:, :, :ds].set(cum.reshape(db, ds, nh).transpose(0, 2, 1))
    tri_u = _lower_tri(page, page).T
    ck3 = cache_k.transpose(0, 2, 3, 1).reshape(n_pool, d, page)
    cv3 = cache_v.transpose(0, 2, 3, 1).reshape(n_pool, d, page)
    clf_t = cache_logf.transpose(0, 2, 1)
    for l in range(n_ret, g_attn.shape[0]):
        q, gate = fox_in(x, g_attn[l], w["wq"][l - n_ret], w["wg"][l - n_ret], w["gq_t"][l - n_ret], w["bd"])
        q3, gate3 = q.reshape(db, ds, d), gate.reshape(db, ds, d)
        if l == n_ret:
            a, k_g, v_g, ck_g, cn_bias = fox_decode_attention_paged(
                page_table, q3, gate3, k_new, v_new, cum_t, tri_u, ck3, cv3, clf_t)
        else:
            a = fox_decode_attention_gathered(q3, gate3, k_new, v_new, cn_bias, k_g, v_g, ck_g)
        x = proj_mlp(x, a.reshape(n, d), w["fox_out"], l - n_ret, g_mlp[l], w["up"], w["down"], l)
    return (x.reshape(db, ds, d), states, k.reshape(db, ds, nh, dh), v.reshape(db, ds, nh, dh),
            logf.reshape(db, ds, nh))


def kernel(x_prompt, x_sample, state_ret, cache_k, cache_v, cache_logf, page_table, meta, g_attn, g_mlp,
           w_ret_in, w_ret_out, g_kv, w_kvf, b_f, g_k, w_fox_qg, g_q, w_fox_out, w_mlp_up, w_mlp_down):
    w = _prep_weights(w_ret_in, w_ret_out, w_kvf, b_f, g_k, w_fox_qg, g_q, w_fox_out, w_mlp_up, w_mlp_down)
    w["g_kv"] = g_kv
    y_p, s_p, k_p, v_p, lf_p = _prompt_forward(x_prompt, meta, g_attn, g_mlp, w)
    y_s, s_s, k_s, v_s, lf_s = _sample_forward(x_sample, state_ret, cache_k, cache_v, cache_logf, page_table,
                                               g_attn, g_mlp, w)
    return (y_p, y_s, s_p, s_s, k_p, v_p, lf_p, k_s, v_s, lf_s)
```
